```python
import jax, jax.numpy as jnp
from jax import lax
import numpy as np

D_MODEL = 1024
BATCH = 2
SEQ = 8192
DEPTH = 1
DEC_BATCH = 128
DEC_SEQ = 1
PAST_LEN = 8192
PAGE_SIZE = 128

HEAD_DIM = 64
N_HG = 4
DIL_GROUPS = ((128, 1), (512, 4), (2048, 16))
N_DIL = len(DIL_GROUPS)
ATT_W = N_DIL * N_HG * HEAD_DIM
ATT_OUT = N_HG * HEAD_DIM
CHUNK = 128
N_GA = 4
GA_CH = 128
D_A = N_GA * GA_CH
D_FF = 2816
CONV_W = 3
QBLK = 128
EPS = 1e-6
IN_SPLITS = (ATT_W, ATT_W, ATT_W, D_A, D_A, D_MODEL, D_MODEL)
IN_COLS = sum(IN_SPLITS)

kernel_name = "gated_chunkgmlp_dilated_swa_convffn_step"


def rmsnorm(x, g):
    xf = x.astype(jnp.float32)
    y = xf * lax.rsqrt(jnp.mean(xf * xf, axis=-1, keepdims=True) + EPS)
    return (y * g.astype(jnp.float32)).astype(x.dtype)


def layernorm(x, g, b):
    xf = x.astype(jnp.float32)
    mu = jnp.mean(xf, axis=-1, keepdims=True)
    var = jnp.mean(jnp.square(xf - mu), axis=-1, keepdims=True)
    y = (xf - mu) * lax.rsqrt(var + EPS)
    return (y * g.astype(jnp.float32) + b.astype(jnp.float32)).astype(x.dtype)


def chunk_spatial_gating(u, va, ln_g, ln_b, w_spatial, b_spatial):
    u = jax.nn.gelu(u)
    vn = layernorm(jax.nn.gelu(va), ln_g, ln_b)
    B, T, _ = u.shape
    n_chunks = -(-T // CHUNK)
    Tp = n_chunks * CHUNK
    vp = jnp.pad(vn, ((0, 0), (0, Tp - T), (0, 0))).reshape(B, n_chunks, CHUNK, N_GA, GA_CH)
    causal = jnp.tril(jnp.ones((CHUNK, CHUNK), dtype=bool))
    w_m = jnp.where(causal[None], w_spatial, jnp.zeros((), w_spatial.dtype)).astype(vp.dtype)
    mixed = jnp.einsum('gts,bnsgc->bntgc', w_m, vp) + b_spatial.T[None, None, :, :, None].astype(vp.dtype)
    mixed = mixed.reshape(B, Tp, D_A)[:, :T]
    return u * mixed, vn


def token_mixing_inputs(x, norm_mix, w_in, ln_g, ln_b, w_spatial, b_spatial):
    xn = rmsnorm(x, norm_mix)
    z = xn @ w_in
    cuts, acc = [], 0
    for width in IN_SPLITS[:-1]:
        acc += width
        cuts.append(acc)
    q, k, v, u, va, ga, gb = jnp.split(z, cuts, axis=-1)
    B, T = x.shape[:2]
    q, k, v = (t.reshape(B, T, N_DIL, N_HG, HEAD_DIM) for t in (q, k, v))
    a_out, v_rows = chunk_spatial_gating(u, va, ln_g, ln_b, w_spatial, b_spatial)
    return a_out, v_rows, q, k, v, ga, gb


def dilated_band_attention(q, k, v, window, dilation):
    B, S, H, E = q.shape
    nk = window // dilation
    M = -(-S // dilation)
    nb = -(-M // QBLK)
    Mp = nb * QBLK
    Sp = Mp * dilation

    def to_strided(t):
        t = jnp.pad(t, ((0, 0), (0, Sp - S), (0, 0), (0, 0)))
        return t.reshape(B, Mp, dilation, H, E).transpose(0, 2, 1, 3, 4)

    def band_keys(t):
        tp = jnp.pad(t, ((0, 0), (0, 0), (QBLK, 0), (0, 0), (0, 0)))
        prev = tp[:, :, :Mp].reshape(B, dilation, nb, QBLK, H, E)
        cur = t.reshape(B, dilation, nb, QBLK, H, E)
        return jnp.concatenate([prev, cur], axis=3)

    qb = to_strided(q).reshape(B, dilation, nb, QBLK, H, E).astype(jnp.float32)
    kb = band_keys(to_strided(k)).astype(jnp.float32)
    vb = band_keys(to_strided(v)).astype(jnp.float32)
    s = jnp.einsum('bdnqhe,bdnkhe->bdnhqk', qb, kb) * (HEAD_DIM ** -0.5)
    qi = jnp.arange(QBLK)[:, None]
    ki = jnp.arange(2 * QBLK)[None, :]
    rel = qi + QBLK - ki
    blk = jnp.arange(nb)[:, None, None]
    valid = (rel >= 0) & (rel <= nk) & (blk * QBLK - QBLK + ki >= 0)
    s = jnp.where(valid[None, None, :, None], s, -jnp.inf)
    lse = jax.nn.logsumexp(s, axis=-1)
    p = jnp.exp(s - lse[..., None])
    o = jnp.einsum('bdnhqk,bdnkhe->bdnqhe', p, vb)
    o = o.reshape(B, dilation, Mp, H, E).transpose(0, 2, 1, 3, 4).reshape(B, Sp, H, E)[:, :S]
    lse = lse.transpose(0, 1, 2, 4, 3).reshape(B, dilation, Mp, H).transpose(0, 2, 1, 3).reshape(B, Sp, H)[:, :S]
    return o, lse


def dilated_step_attention(q, k_new, v_new, kv_buf, window, dilation):
    T = q.shape[1]
    L = kv_buf.shape[1]
    nk = window // dilation
    kv_all = jnp.concatenate([kv_buf.astype(q.dtype), jnp.stack([k_new, v_new], axis=2)], axis=1)
    idx = L + jnp.arange(T)[:, None] - dilation * jnp.arange(nk + 1)[None, :]
    valid = idx >= 0
    g = jnp.take(kv_all, jnp.maximum(idx, 0), axis=1).astype(jnp.float32)
    s = jnp.einsum('bthe,btjhe->bthj', q.astype(jnp.float32), g[:, :, :, 0]) * (HEAD_DIM ** -0.5)
    s = jnp.where(valid[None, :, None, :], s, -jnp.inf)
    lse = jax.nn.logsumexp(s, axis=-1)
    p = jnp.exp(s - lse[..., None])
    o = jnp.einsum('bthj,btjhe->bthe', p, g[:, :, :, 1])
    return o, lse


def combine_dilations(outs, lses, dtype):
    w = jax.nn.softmax(jnp.stack(lses), axis=0)
    o = jnp.sum(w[..., None] * jnp.stack(outs), axis=0)
    return o.reshape(o.shape[0], o.shape[1], ATT_OUT).astype(dtype)


def merge_and_ffn(x, a_out, b_out, ga, gb, conv_hist, w_proj_a, w_proj_b, w_out,
                  norm_ffn, w_up, conv_w, conv_b, w_down):
    merged = jax.nn.sigmoid(ga) * (a_out @ w_proj_a) + jax.nn.sigmoid(gb) * (b_out @ w_proj_b)
    h = x + merged @ w_out
    up = rmsnorm(h, norm_ffn) @ w_up
    T = up.shape[1]
    xp = jnp.concatenate([conv_hist.astype(up.dtype), up], axis=1)
    c = conv_b + conv_w[CONV_W - 1] * xp[:, CONV_W - 1:CONV_W - 1 + T]
    for i in range(CONV_W - 1):
        c = c + conv_w[i] * xp[:, i:i + T]
    gate, val = jnp.split(c, 2, axis=-1)
    h = h + (jax.nn.gelu(gate) * val) @ w_down
    return h, xp[:, -(CONV_W - 1):]


def setup_inputs(seed: int = 0) -> dict:
    key = jax.random.key(seed)
    ks = jax.random.split(key, 24)
    f32 = jnp.float32

    def nrm(k, shape, scale):
        return scale * jax.random.normal(k, shape, f32)

    cache_kv = [nrm(ks[2 + i], (DEPTH, DEC_BATCH, min(w, PAST_LEN), 2, N_HG, HEAD_DIM), 1.0)
                for i, (w, _) in enumerate(DIL_GROUPS)]
    return {
        "x_prompt": nrm(ks[0], (BATCH, SEQ, D_MODEL), 1.0),
        "x_sample": nrm(ks[1], (DEC_BATCH, DEC_SEQ, D_MODEL), 1.0),
        "cache_kv_w128": cache_kv[0],
        "cache_kv_w512": cache_kv[1],
        "cache_kv_w2048": cache_kv[2],
        "state_conv_ffn": nrm(ks[5], (DEPTH, DEC_BATCH, CONV_W - 1, 2 * D_FF), 1.0),
        "norm_mix": 1.0 + nrm(ks[6], (DEPTH, D_MODEL), 0.05),
        "w_in": nrm(ks[7], (DEPTH, D_MODEL, IN_COLS), D_MODEL ** -0.5),
        "ln_v_gain": 1.0 + nrm(ks[8], (DEPTH, D_A), 0.05),
        "ln_v_bias": nrm(ks[9], (DEPTH, D_A), 0.01),
        "w_spatial": nrm(ks[10], (DEPTH, N_GA, CHUNK, CHUNK), CHUNK ** -0.5),
        "b_spatial": 1.0 + nrm(ks[11], (DEPTH, N_GA, CHUNK), 0.1),
        "w_proj_a": nrm(ks[12], (DEPTH, D_A, D_MODEL), D_A ** -0.5),
        "w_proj_b": nrm(ks[13], (DEPTH, ATT_OUT, D_MODEL), ATT_OUT ** -0.5),
        "w_out": nrm(ks[14], (DEPTH, D_MODEL, D_MODEL), D_MODEL ** -0.5),
        "norm_ffn": 1.0 + nrm(ks[15], (DEPTH, D_MODEL), 0.05),
        "w_up": nrm(ks[16], (DEPTH, D_MODEL, 2 * D_FF), D_MODEL ** -0.5),
        "conv_w": nrm(ks[17], (DEPTH, CONV_W, 2 * D_FF), CONV_W ** -0.5),
        "conv_b": nrm(ks[18], (DEPTH, 2 * D_FF), 0.01),
        "w_down": nrm(ks[19], (DEPTH, D_FF, D_MODEL), D_FF ** -0.5),
        "norm_final": 1.0 + nrm(ks[20], (D_MODEL,), 0.05),
    }


def reference(x_prompt, x_sample, cache_kv_w128, cache_kv_w512, cache_kv_w2048, state_conv_ffn,
              norm_mix, w_in, ln_v_gain, ln_v_bias, w_spatial, b_spatial, w_proj_a, w_proj_b,
              w_out, norm_ffn, w_up, conv_w, conv_b, w_down, norm_final):
    S = x_prompt.shape[1]
    kv_caches = (cache_kv_w128, cache_kv_w512, cache_kv_w2048)
    h_p, h_s = x_prompt, x_sample
    kv_p_rows = [[] for _ in DIL_GROUPS]
    kv_s_rows = [[] for _ in DIL_GROUPS]
    v_p_rows, v_s_rows, conv_p_rows, conv_s_rows = [], [], [], []
    last_chunk_start = ((S - 1) // CHUNK) * CHUNK
    for l in range(DEPTH):
        a_p, vr_p, q, k, v, ga, gb = token_mixing_inputs(
            h_p, norm_mix[l], w_in[l], ln_v_gain[l], ln_v_bias[l], w_spatial[l], b_spatial[l])
        outs, lses = [], []
        for g, (win, dil) in enumerate(DIL_GROUPS):
            o, lse = dilated_band_attention(q[:, :, g], k[:, :, g], v[:, :, g], win, dil)
            outs.append(o)
            lses.append(lse)
            kv_p_rows[g].append(jnp.stack([k[:, :, g], v[:, :, g]], axis=2)[:, S - min(win, S):])
        b_p = combine_dilations(outs, lses, h_p.dtype)
        zero_hist = jnp.zeros((h_p.shape[0], CONV_W - 1, 2 * D_FF), h_p.dtype)
        h_p, conv_p = merge_and_ffn(h_p, a_p, b_p, ga, gb, zero_hist, w_proj_a[l], w_proj_b[l], w_out[l],
                                    norm_ffn[l], w_up[l], conv_w[l], conv_b[l], w_down[l])
        v_p_rows.append(vr_p[:, last_chunk_start:])
        conv_p_rows.append(conv_p)

        a_s, vr_s, q, k, v, ga, gb = token_mixing_inputs(
            h_s, norm_mix[l], w_in[l], ln_v_gain[l], ln_v_bias[l], w_spatial[l], b_spatial[l])
        outs, lses = [], []
        for g, (win, dil) in enumerate(DIL_GROUPS):
            o, lse = dilated_step_attention(q[:, :, g], k[:, :, g], v[:, :, g], kv_caches[g][l], win, dil)
            outs.append(o)
            lses.append(lse)
            kv_s_rows[g].append(jnp.stack([k[:, :, g], v[:, :, g]], axis=2))
        b_s = combine_dilations(outs, lses, h_s.dtype)
        h_s, conv_s = merge_and_ffn(h_s, a_s, b_s, ga, gb, state_conv_ffn[l], w_proj_a[l], w_proj_b[l],
                                    w_out[l], norm_ffn[l], w_up[l], conv_w[l], conv_b[l], w_down[l])
        v_s_rows.append(vr_s)
        conv_s_rows.append(conv_s)

    y_prompt = rmsnorm(h_p, norm_final)
    y_sample = rmsnorm(h_s, norm_final)
    kv_w128_prompt = jnp.stack(kv_p_rows[0])
    kv_w128_sample = jnp.stack(kv_s_rows[0])
    kv_w512_prompt = jnp.stack(kv_p_rows[1])
    kv_w512_sample = jnp.stack(kv_s_rows[1])
    kv_w2048_prompt = jnp.stack(kv_p_rows[2])
    kv_w2048_sample = jnp.stack(kv_s_rows[2])
    v_chunk_prompt = jnp.stack(v_p_rows)
    v_chunk_sample = jnp.stack(v_s_rows)
    conv_prompt = jnp.stack(conv_p_rows)
    conv_sample = jnp.stack(conv_s_rows)
    return (y_prompt, y_sample, kv_w128_prompt, kv_w128_sample, kv_w512_prompt, kv_w512_sample,
            kv_w2048_prompt, kv_w2048_sample, v_chunk_prompt, v_chunk_sample, conv_prompt, conv_sample)
```

```python
import functools
import math

import jax
import jax.numpy as jnp
from jax import lax
from jax.experimental import pallas as pl
from jax.experimental.pallas import tpu as pltpu

F32 = jnp.float32
BF16 = jnp.bfloat16

HEAD_DIM = 64
N_HG = 4
ATT_G = N_HG * HEAD_DIM
DIL_GROUPS = ((128, 1), (512, 4), (2048, 16))
N_DIL = len(DIL_GROUPS)
ATT_W = N_DIL * ATT_G
BAND = 128
CHUNK = 128
N_GA = 4
GA_CH = 128
D_A = N_GA * GA_CH
CONV_W = 3
EPS = 1e-6
GELU_C = math.sqrt(2.0 / math.pi)

VMEM_LIMIT_BYTES = 56 * 1024 * 1024
FF_BLK = 256


def _gelu(x):
    return x * (0.5 * (1.0 + jnp.tanh(GELU_C * (x + 0.044715 * (x * x * x)))))


def _rmsnorm(x, g):
    return x * lax.rsqrt(jnp.mean(x * x, axis=-1, keepdims=True) + EPS) * g


def _layernorm(x, g, b):
    mu = jnp.mean(x, axis=-1, keepdims=True)
    xc = x - mu
    var = jnp.mean(xc * xc, axis=-1, keepdims=True)
    return xc * lax.rsqrt(var + EPS) * g + b


def _dot(a, b):
    return jnp.dot(a, b, preferred_element_type=F32)


def _dot_nt(a, b):
    return lax.dot_general(a, b, (((1,), (1,)), ((), ())), preferred_element_type=F32)


def _const_spec(shape):
    nd = len(shape)
    return pl.BlockSpec(shape, lambda *_: (0,) * nd, pipeline_mode=pl.Buffered(1))


def _params(*sem):
    return pltpu.CompilerParams(dimension_semantics=sem, vmem_limit_bytes=VMEM_LIMIT_BYTES)


def _project(xn, w_ref, lo, hi):
    return _dot(xn, w_ref[:, lo:hi])


def _gates_and_branch_a(xn, w_ref, a_bf16, wpa_ref, d_model):
    base = 3 * ATT_W + 2 * D_A
    pa = _dot(a_bf16, wpa_ref[...])
    ga = _project(xn, w_ref, base, base + d_model)
    gb = _project(xn, w_ref, base + d_model, base + 2 * d_model)
    return jax.nn.sigmoid(ga) * pa, jax.nn.sigmoid(gb)


def _combine_groups(o_list):
    lses = [o[:, ATT_G:] for o in o_list]
    m = functools.reduce(jnp.maximum, lses)
    ws = [jnp.exp(l - m) for l in lses]
    den = functools.reduce(lambda a, b: a + b, ws)
    num = functools.reduce(lambda a, b: a + b, [w * o[:, :ATT_G] for w, o in zip(ws, o_list)])
    return num / den


def _conv_ffn(hn, prev1_fn, prev2_fn, wup_ref, cw_ref, cb_ref, wdn_ref, up_sink, d_ff):
    acc = None
    for j in range(d_ff // FF_BLK):
        halves = []
        for base in (0, d_ff):
            lo = base + j * FF_BLK
            u = _dot(hn, wup_ref[:, lo:lo + FF_BLK])
            c = (cb_ref[:, lo:lo + FF_BLK] + cw_ref[2:3, lo:lo + FF_BLK] * u
                 + cw_ref[0:1, lo:lo + FF_BLK] * prev2_fn(u, lo)
                 + cw_ref[1:2, lo:lo + FF_BLK] * prev1_fn(u, lo))
            up_sink(u, lo)
            halves.append(c)
        act = (_gelu(halves[0]) * halves[1]).astype(BF16)
        part = _dot(act, wdn_ref[j * FF_BLK:(j + 1) * FF_BLK, :])
        acc = part if acc is None else acc + part
    return acc


def _mix_in_kernel(x_ref, g_ref, w_ref, lng_ref, lnb_ref, wsp_ref, bsp_ref, wpa_ref,
                   qkv_ref, kvt_ref, ma_ref, sgb_ref, vn_ref, *, d_model):
    t = x_ref.shape[1]
    xn = _rmsnorm(x_ref[0], g_ref[...]).astype(BF16)
    q = _project(xn, w_ref, 0, ATT_W) * (HEAD_DIM ** -0.5)
    qkv_ref[0, :, 0:ATT_W] = q.astype(BF16)
    kv = _project(xn, w_ref, ATT_W, 3 * ATT_W)
    qkv_ref[0, :, ATT_W:3 * ATT_W] = kv.astype(BF16)
    kvt_ref[0] = kv

    u = _project(xn, w_ref, 3 * ATT_W, 3 * ATT_W + D_A)
    va = _project(xn, w_ref, 3 * ATT_W + D_A, 3 * ATT_W + 2 * D_A)
    vn = _layernorm(_gelu(va), lng_ref[...], lnb_ref[...])
    vn_ref[0] = vn[t - CHUNK:, :]
    vnb = vn.astype(BF16)
    row = lax.broadcasted_iota(jnp.int32, (CHUNK, CHUNK), 0)
    col = lax.broadcasted_iota(jnp.int32, (CHUNK, CHUNK), 1)
    w_m = [jnp.where(row >= col, wsp_ref[g], 0.0).astype(BF16) for g in range(N_GA)]
    rows = []
    for c in range(t // CHUNK):
        parts = [_dot(w_m[g], vnb[c * CHUNK:(c + 1) * CHUNK, g * GA_CH:(g + 1) * GA_CH]) + bsp_ref[g]
                 for g in range(N_GA)]
        rows.append(jnp.concatenate(parts, axis=1))
    mixed = jnp.concatenate(rows, axis=0)
    a = (_gelu(u) * mixed).astype(BF16)
    ma, sgb = _gates_and_branch_a(xn, w_ref, a, wpa_ref, d_model)
    ma_ref[0] = ma
    sgb_ref[0] = sgb


def _mix_in(x, norm_mix, w_in, ln_g, ln_b, w_sp, b_sp, w_pa, *, tile, tail):
    b, s, d = x.shape
    n_t = s // tile
    tail_blocks = tail // tile
    in_cols = w_in.shape[1]
    kern = functools.partial(_mix_in_kernel, d_model=d)
    return pl.pallas_call(
        kern,
        grid=(b, n_t),
        in_specs=[
            pl.BlockSpec((1, tile, d), lambda i, j: (i, j, 0)),
            _const_spec((1, d)),
            _const_spec((d, in_cols)),
            _const_spec((1, D_A)),
            _const_spec((1, D_A)),
            _const_spec((N_GA, CHUNK, CHUNK)),
            _const_spec((N_GA, CHUNK, 1)),
            _const_spec((D_A, d)),
        ],
        out_specs=[
            pl.BlockSpec((1, tile, 3 * ATT_W), lambda i, j: (i, j, 0)),
            pl.BlockSpec((1, tile, 2 * ATT_W), lambda i, j: (i, jnp.maximum(j - (n_t - tail_blocks), 0), 0)),
            pl.BlockSpec((1, tile, d), lambda i, j: (i, j, 0)),
            pl.BlockSpec((1, tile, d), lambda i, j: (i, j, 0)),
            pl.BlockSpec((1, CHUNK, D_A), lambda i, j: (i, 0, 0)),
        ],
        out_shape=[
            jax.ShapeDtypeStruct((b, s, 3 * ATT_W), BF16),
            jax.ShapeDtypeStruct((b, tail, 2 * ATT_W), F32),
            jax.ShapeDtypeStruct((b, s, d), F32),
            jax.ShapeDtypeStruct((b, s, d), F32),
            jax.ShapeDtypeStruct((b, CHUNK, D_A), F32),
        ],
        compiler_params=_params("arbitrary", "arbitrary"),
        name="prompt_mix_in",
    )(x, norm_mix, w_in, ln_g, ln_b, w_sp, b_sp, w_pa)


def _band_attn_kernel(cur_ref, halo_ref, o_ref):
    rows = cur_ref.shape[2]
    first = pl.program_id(2) == 0
    qi = lax.broadcasted_iota(jnp.int32, (BAND, BAND), 0)
    ki = lax.broadcasted_iota(jnp.int32, (BAND, BAND), 1)
    cur_ok = ki <= qi
    prev_ok = ki >= qi
    prev_ok_first = jnp.logical_and(prev_ok, jnp.logical_not(first))
    for qb in range(rows // BAND):
        r0 = qb * BAND
        outs, lses = [], []
        for h in range(N_HG):
            cq = slice(h * HEAD_DIM, (h + 1) * HEAD_DIM)
            ck = slice(ATT_G + h * HEAD_DIM, ATT_G + (h + 1) * HEAD_DIM)
            cv = slice(2 * ATT_G + h * HEAD_DIM, 2 * ATT_G + (h + 1) * HEAD_DIM)
            q = cur_ref[0, 0, r0:r0 + BAND, cq]
            k_c = cur_ref[0, 0, r0:r0 + BAND, ck]
            v_c = cur_ref[0, 0, r0:r0 + BAND, cv]
            if qb == 0:
                k_p, v_p, p_mask = halo_ref[0, 0, :, ck], halo_ref[0, 0, :, cv], prev_ok_first
            else:
                k_p, v_p, p_mask = cur_ref[0, 0, r0 - BAND:r0, ck], cur_ref[0, 0, r0 - BAND:r0, cv], prev_ok
            s_c = jnp.where(cur_ok, _dot_nt(q, k_c), -jnp.inf)
            s_p = jnp.where(p_mask, _dot_nt(q, k_p), -jnp.inf)
            m = jnp.maximum(jnp.max(s_c, axis=-1, keepdims=True), jnp.max(s_p, axis=-1, keepdims=True))
            p_c = jnp.exp(s_c - m)
            p_p = jnp.exp(s_p - m)
            l = jnp.sum(p_c, axis=-1, keepdims=True) + jnp.sum(p_p, axis=-1, keepdims=True)
            acc = _dot(p_c.astype(BF16), v_c) + _dot(p_p.astype(BF16), v_p)
            outs.append(acc / l)
            lses.append(jnp.broadcast_to(m + jnp.log(l), (BAND, HEAD_DIM)))
        o_ref[0, 0, r0:r0 + BAND, :] = jnp.concatenate(outs + lses, axis=1)


def _band_attn(qkv_g, *, rows):
    b, d, m, w = qkv_g.shape
    per = rows // BAND
    return pl.pallas_call(
        _band_attn_kernel,
        grid=(b, d, m // rows),
        in_specs=[
            pl.BlockSpec((1, 1, rows, w), lambda i, r, j: (i, r, j, 0)),
            pl.BlockSpec((1, 1, BAND, w), lambda i, r, j: (i, r, jnp.maximum(j * per - 1, 0), 0)),
        ],
        out_specs=pl.BlockSpec((1, 1, rows, 2 * ATT_G), lambda i, r, j: (i, r, j, 0)),
        out_shape=jax.ShapeDtypeStruct((b, d, m, 2 * ATT_G), F32),
        compiler_params=_params("arbitrary", "arbitrary", "arbitrary"),
        name=f"prompt_band_attn_d{d}",
    )(qkv_g, qkv_g)


def _merge_ffn_kernel(x_ref, ma_ref, sgb_ref, o0_ref, o1_ref, o2_ref, wpb_ref, wout_ref, nf_ref,
                      wup_ref, cw_ref, cb_ref, wdn_ref, nfin_ref, y_ref, ct_ref, carry_ref, *, d_ff):
    t = x_ref.shape[1]

    @pl.when(pl.program_id(1) == 0)
    def _():
        carry_ref[...] = jnp.zeros_like(carry_ref)

    b_out = _combine_groups([o0_ref[0], o1_ref[0], o2_ref[0]]).astype(BF16)
    merged = ma_ref[0] + sgb_ref[0] * _dot(b_out, wpb_ref[...])
    h = x_ref[0] + _dot(merged.astype(BF16), wout_ref[...])
    hn = _rmsnorm(h, nf_ref[...]).astype(BF16)
    row = lax.broadcasted_iota(jnp.int32, (t, FF_BLK), 0)

    def prev1(u, lo):
        return jnp.where(row == 0, carry_ref[7:8, lo:lo + FF_BLK], pltpu.roll(u, 1, 0))

    def prev2(u, lo):
        rolled = pltpu.roll(u, 2, 0)
        rolled = jnp.where(row == 1, carry_ref[7:8, lo:lo + FF_BLK], rolled)
        return jnp.where(row == 0, carry_ref[6:7, lo:lo + FF_BLK], rolled)

    def sink(u, lo):
        carry_ref[:, lo:lo + FF_BLK] = u[t - 8:, :]

    ffn = _conv_ffn(hn, prev1, prev2, wup_ref, cw_ref, cb_ref, wdn_ref, sink, d_ff)
    ct_ref[0] = carry_ref[8 - (CONV_W - 1):, :]
    y_ref[0] = _rmsnorm(h + ffn, nfin_ref[...])


def _merge_ffn(x, ma, sgb, o_list, w_pb, w_out, norm_ffn, w_up, conv_w, conv_b, w_dn, norm_final, *, tile):
    b, s, d = x.shape
    d_ff = w_dn.shape[0]
    tok = lambda w: pl.BlockSpec((1, tile, w), lambda i, j: (i, j, 0))
    kern = functools.partial(_merge_ffn_kernel, d_ff=d_ff)
    return pl.pallas_call(
        kern,
        grid=(b, s // tile),
        in_specs=[tok(d), tok(d), tok(d), tok(2 * ATT_G), tok(2 * ATT_G), tok(2 * ATT_G),
                  _const_spec((ATT_G, d)), _const_spec((d, d)), _const_spec((1, d)),
                  _const_spec((d, 2 * d_ff)), _const_spec((CONV_W, 2 * d_ff)), _const_spec((1, 2 * d_ff)),
                  _const_spec((d_ff, d)), _const_spec((1, d))],
        out_specs=[tok(d), pl.BlockSpec((1, CONV_W - 1, 2 * d_ff), lambda i, j: (i, 0, 0))],
        out_shape=[jax.ShapeDtypeStruct((b, s, d), F32),
                   jax.ShapeDtypeStruct((b, CONV_W - 1, 2 * d_ff), F32)],
        scratch_shapes=[pltpu.VMEM((8, 2 * d_ff), F32)],
        compiler_params=_params("arbitrary", "arbitrary"),
        name="prompt_merge_ffn",
    )(x, ma, sgb, *o_list, w_pb, w_out, norm_ffn, w_up, conv_w, conv_b, w_dn, norm_final)


def _sample_in_kernel(x_ref, g_ref, w_ref, lng_ref, lnb_ref, w00_ref, b0_ref, wpa_ref,
                      q_ref, kv_ref, vn_ref, ma_ref, sgb_ref, *, d_model):
    xn = _rmsnorm(x_ref[...], g_ref[...]).astype(BF16)
    q_ref[...] = _project(xn, w_ref, 0, ATT_W) * (HEAD_DIM ** -0.5)
    kv_ref[...] = _project(xn, w_ref, ATT_W, 3 * ATT_W)
    u = _project(xn, w_ref, 3 * ATT_W, 3 * ATT_W + D_A)
    va = _project(xn, w_ref, 3 * ATT_W + D_A, 3 * ATT_W + 2 * D_A)
    vn = _layernorm(_gelu(va), lng_ref[...], lnb_ref[...])
    vn_ref[...] = vn
    mixed = w00_ref[...] * vn + b0_ref[...]
    a = (_gelu(u) * mixed).astype(BF16)
    ma, sgb = _gates_and_branch_a(xn, w_ref, a, wpa_ref, d_model)
    ma_ref[...] = ma
    sgb_ref[...] = sgb


def _sample_in(x, norm_mix, w_in, ln_g, ln_b, w00, b0, w_pa):
    n, d = x.shape
    in_cols = w_in.shape[1]
    kern = functools.partial(_sample_in_kernel, d_model=d)
    full = lambda *shape: pl.BlockSpec(shape, lambda: (0,) * len(shape))
    return pl.pallas_call(
        kern,
        in_specs=[full(n, d), full(1, d), full(d, in_cols), full(1, D_A), full(1, D_A),
                  full(1, D_A), full(1, D_A), full(D_A, d)],
        out_specs=[full(n, ATT_W), full(n, 2 * ATT_W), full(n, D_A), full(n, d), full(n, d)],
        out_shape=[jax.ShapeDtypeStruct((n, ATT_W), F32), jax.ShapeDtypeStruct((n, 2 * ATT_W), F32),
                   jax.ShapeDtypeStruct((n, D_A), F32), jax.ShapeDtypeStruct((n, d), F32),
                   jax.ShapeDtypeStruct((n, d), F32)],
        compiler_params=pltpu.CompilerParams(vmem_limit_bytes=VMEM_LIMIT_BYTES),
        name="sample_in",
    )(x, norm_mix, w_in, ln_g, ln_b, w00, b0, w_pa)


def _sample_attn_kernel(q_ref, kn_ref, vn_ref, c0_ref, c1_ref, c2_ref, o_ref):
    ms, ls, accs = [], [], []
    for g, c_ref in enumerate((c0_ref, c1_ref, c2_ref)):
        q = q_ref[:, g:g + 1]
        k_new = kn_ref[:, g:g + 1]
        v_new = vn_ref[:, g:g + 1]
        s = jnp.sum(c_ref[:, :, 0] * q, axis=-1, keepdims=True)
        s_new = jnp.sum(k_new * q, axis=-1, keepdims=True)
        m = jnp.maximum(jnp.max(s, axis=1, keepdims=True), s_new)
        p = jnp.exp(s - m)
        p_new = jnp.exp(s_new - m)
        ls.append(jnp.sum(p, axis=1, keepdims=True) + p_new)
        accs.append(jnp.sum(p * c_ref[:, :, 1], axis=1, keepdims=True) + p_new * v_new)
        ms.append(m)
    m_all = functools.reduce(jnp.maximum, ms)
    scale = [jnp.exp(m - m_all) for m in ms]
    num = functools.reduce(lambda a, b: a + b, [sc * acc for sc, acc in zip(scale, accs)])
    den = functools.reduce(lambda a, b: a + b, [sc * l for sc, l in zip(scale, ls)])
    o_ref[...] = num / den


def _sample_attn(q, k_new, v_new, caches, *, bb):
    n = q.shape[0]
    tok = pl.BlockSpec((bb, N_DIL, N_HG, HEAD_DIM), lambda i: (i, 0, 0, 0))
    cache_spec = pl.BlockSpec((bb, BAND, None, 2, N_HG, HEAD_DIM), lambda i: (i, 0, 0, 0, 0, 0))
    return pl.pallas_call(
        _sample_attn_kernel,
        grid=(n // bb,),
        in_specs=[tok, tok, tok, cache_spec, cache_spec, cache_spec],
        out_specs=pl.BlockSpec((bb, 1, N_HG, HEAD_DIM), lambda i: (i, 0, 0, 0)),
        out_shape=jax.ShapeDtypeStruct((n, 1, N_HG, HEAD_DIM), F32),
        compiler_params=_params("arbitrary"),
        name="sample_cache_attn",
    )(q, k_new, v_new, *caches)


def _sample_ffn_kernel(x_ref, ma_ref, sgb_ref, b_ref, h0_ref, h1_ref, wpb_ref, wout_ref, nf_ref,
                       wup_ref, cw_ref, cb_ref, wdn_ref, nfin_ref, y_ref, up_ref, *, d_ff):
    merged = ma_ref[...] + sgb_ref[...] * _dot(b_ref[...].astype(BF16), wpb_ref[...])
    h = x_ref[...] + _dot(merged.astype(BF16), wout_ref[...])
    hn = _rmsnorm(h, nf_ref[...]).astype(BF16)

    def sink(u, lo):
        up_ref[:, lo:lo + FF_BLK] = u

    ffn = _conv_ffn(hn, lambda u, lo: h1_ref[:, lo:lo + FF_BLK], lambda u, lo: h0_ref[:, lo:lo + FF_BLK],
                    wup_ref, cw_ref, cb_ref, wdn_ref, sink, d_ff)
    y_ref[...] = _rmsnorm(h + ffn, nfin_ref[...])


def _sample_ffn(x, ma, sgb, b_out, hist0, hist1, w_pb, w_out, norm_ffn, w_up, conv_w, conv_b, w_dn, norm_final):
    n, d = x.shape
    d_ff = w_dn.shape[0]
    kern = functools.partial(_sample_ffn_kernel, d_ff=d_ff)
    full = lambda *shape: pl.BlockSpec(shape, lambda: (0,) * len(shape))
    return pl.pallas_call(
        kern,
        in_specs=[full(n, d), full(n, d), full(n, d), full(n, ATT_G), full(n, 2 * d_ff), full(n, 2 * d_ff),
                  full(ATT_G, d), full(d, d), full(1, d), full(d, 2 * d_ff), full(CONV_W, 2 * d_ff),
                  full(1, 2 * d_ff), full(d_ff, d), full(1, d)],
        out_specs=[full(n, d), full(n, 2 * d_ff)],
        out_shape=[jax.ShapeDtypeStruct((n, d), F32), jax.ShapeDtypeStruct((n, 2 * d_ff), F32)],
        compiler_params=pltpu.CompilerParams(vmem_limit_bytes=VMEM_LIMIT_BYTES),
        name="sample_merge_ffn",
    )(x, ma, sgb, b_out, hist0, hist1, w_pb, w_out, norm_ffn, w_up, conv_w, conv_b, w_dn, norm_final)


def kernel(x_prompt, x_sample, cache_kv_w128, cache_kv_w512, cache_kv_w2048, state_conv_ffn, norm_mix, w_in, ln_v_gain, ln_v_bias, w_spatial, b_spatial, w_proj_a, w_proj_b, w_out, norm_ffn, w_up, conv_w, conv_b, w_down, norm_final):
    b, s, d = x_prompt.shape
    n_s, t_s, _ = x_sample.shape
    depth = w_in.shape[0]
    caches = (cache_kv_w128, cache_kv_w512, cache_kv_w2048)
    max_win = max(w for w, _ in DIL_GROUPS)
    assert depth == 1 and t_s == 1, "one layer, one new token per sample sequence"
    assert s % max_win == 0, "prompt length must cover whole strided query blocks of every dilation group"
    for (win, _), c in zip(DIL_GROUPS, caches):
        assert c.shape[2] == win, "cache must hold the whole window"

    w_in_b = w_in[0].astype(BF16)
    w_pa_b = w_proj_a[0].astype(BF16)
    w_pb_b = w_proj_b[0].astype(BF16)
    w_out_b = w_out[0].astype(BF16)
    w_up_b = w_up[0].astype(BF16)
    w_dn_b = w_down[0].astype(BF16)
    ln_g = ln_v_gain[0][None]
    ln_b = ln_v_bias[0][None]
    conv_b2 = conv_b[0][None]
    nfin = norm_final[None]

    qkv, kv_tail, ma, sgb, vn_last = _mix_in(
        x_prompt, norm_mix, w_in_b, ln_g, ln_b, w_spatial[0], b_spatial[0][:, :, None], w_pa_b,
        tile=512, tail=max_win)
    o_list = []
    for g, (win, dil) in enumerate(DIL_GROUPS):
        cols = [qkv[:, :, sec * ATT_W + g * ATT_G: sec * ATT_W + (g + 1) * ATT_G] for sec in range(3)]
        qkv_g = jnp.concatenate(cols, axis=-1).reshape(b, s // dil, dil, 3 * ATT_G).transpose(0, 2, 1, 3)
        o_g = _band_attn(qkv_g, rows=min(512, s // dil))
        o_list.append(o_g.transpose(0, 2, 1, 3).reshape(b, s, 2 * ATT_G))
    y_prompt, conv_prompt = _merge_ffn(x_prompt, ma, sgb, o_list, w_pb_b, w_out_b, norm_ffn, w_up_b,
                                       conv_w[0], conv_b2, w_dn_b, nfin, tile=256)

    kv_prompt = []
    for g, (win, _) in enumerate(DIL_GROUPS):
        k_g = kv_tail[:, max_win - win:, g * ATT_G:(g + 1) * ATT_G]
        v_g = kv_tail[:, max_win - win:, ATT_W + g * ATT_G: ATT_W + (g + 1) * ATT_G]
        kv_prompt.append(jnp.stack([k_g, v_g], axis=2).reshape(1, b, win, 2, N_HG, HEAD_DIM))

    w00 = jnp.repeat(w_spatial[0][:, 0, 0], GA_CH)[None]
    b0 = jnp.repeat(b_spatial[0][:, 0], GA_CH)[None]
    q_s, kv_s, vn_s, ma_s, sgb_s = _sample_in(x_sample[:, 0], norm_mix, w_in_b, ln_g, ln_b, w00, b0, w_pa_b)
    heads = lambda z: z.reshape(n_s, N_DIL, N_HG, HEAD_DIM)
    cache_views = [c[0].reshape(n_s, win // dil, dil, 2, N_HG, HEAD_DIM) for c, (win, dil) in zip(caches, DIL_GROUPS)]
    b_s = _sample_attn(heads(q_s), heads(kv_s[:, :ATT_W]), heads(kv_s[:, ATT_W:]), cache_views, bb=4)
    hist = state_conv_ffn[0]
    y_s, up_s = _sample_ffn(x_sample[:, 0], ma_s, sgb_s, b_s.reshape(n_s, ATT_G), hist[:, 0], hist[:, 1],
                            w_pb_b, w_out_b, norm_ffn, w_up_b, conv_w[0], conv_b2, w_dn_b, nfin)

    kv_sample = [jnp.stack([kv_s[:, g * ATT_G:(g + 1) * ATT_G], kv_s[:, ATT_W + g * ATT_G: ATT_W + (g + 1) * ATT_G]],
                           axis=1).reshape(1, n_s, 1, 2, N_HG, HEAD_DIM) for g in range(N_DIL)]
    conv_sample = jnp.concatenate([hist[:, CONV_W - 2:], up_s[:, None]], axis=1)[None]

    return (y_prompt, y_s[:, None], kv_prompt[0], kv_sample[0], kv_prompt[1], kv_sample[1],
            kv_prompt[2], kv_sample[2], vn_last[None], vn_s[None, :, None], conv_prompt[None], conv_sample)
```

```python
import functools
import math

import jax
import jax.numpy as jnp
from jax import lax
from jax.experimental import pallas as pl
from jax.experimental.pallas import tpu as pltpu

F32 = jnp.float32
BF16 = jnp.bfloat16

HEAD_DIM = 64
N_HG = 4
ATT_G = N_HG * HEAD_DIM
DIL_GROUPS = ((128, 1), (512, 4), (2048, 16))
N_DIL = len(DIL_GROUPS)
ATT_W = N_DIL * ATT_G
BAND = 128
CHUNK = 128
N_GA = 4
GA_CH = 128
D_A = N_GA * GA_CH
CONV_W = 3
EPS = 1e-6
GELU_C = math.sqrt(2.0 / math.pi)

VMEM_LIMIT_BYTES = 56 * 1024 * 1024
FF_BLK = 256


def _gelu(x):
    return x * (0.5 * (1.0 + jnp.tanh(GELU_C * (x + 0.044715 * (x * x * x)))))


def _rmsnorm(x, g):
    return x * lax.rsqrt(jnp.mean(x * x, axis=-1, keepdims=True) + EPS) * g


def _layernorm(x, g, b):
    mu = jnp.mean(x, axis=-1, keepdims=True)
    xc = x - mu
    var = jnp.mean(xc * xc, axis=-1, keepdims=True)
    return xc * lax.rsqrt(var + EPS) * g + b


def _dot(a, b):
    return jnp.dot(a, b, preferred_element_type=F32)


def _dot_nt(a, b):
    return lax.dot_general(a, b, (((1,), (1,)), ((), ())), preferred_element_type=F32)


def _const_spec(shape):
    nd = len(shape)
    return pl.BlockSpec(shape, lambda *_: (0,) * nd, pipeline_mode=pl.Buffered(1))


def _params(*sem):
    return pltpu.CompilerParams(dimension_semantics=sem, vmem_limit_bytes=VMEM_LIMIT_BYTES)


def _project(xn, w_ref, lo, hi):
    return _dot(xn, w_ref[:, lo:hi])


def _gates_and_branch_a(xn, w_ref, a_bf16, wpa_ref, d_model):
    base = 3 * ATT_W + 2 * D_A
    pa = _dot(a_bf16, wpa_ref[...])
    ga = _project(xn, w_ref, base, base + d_model)
    gb = _project(xn, w_ref, base + d_model, base + 2 * d_model)
    return jax.nn.sigmoid(ga) * pa, jax.nn.sigmoid(gb)


def _combine_groups(o_list):
    lses = [o[:, ATT_G:] for o in o_list]
    m = functools.reduce(jnp.maximum, lses)
    ws = [jnp.exp(l - m) for l in lses]
    den = functools.reduce(lambda a, b: a + b, ws)
    num = functools.reduce(lambda a, b: a + b, [w * o[:, :ATT_G] for w, o in zip(ws, o_list)])
    return num / den


def _conv_ffn(hn, prev1_fn, prev2_fn, wup_ref, cw_ref, cb_ref, wdn_ref, up_sink, d_ff):
    acc = None
    for j in range(d_ff // FF_BLK):
        halves = []
        for base in (0, d_ff):
            lo = base + j * FF_BLK
            u = _dot(hn, wup_ref[:, lo:lo + FF_BLK])
            c = (cb_ref[:, lo:lo + FF_BLK] + cw_ref[2:3, lo:lo + FF_BLK] * u
                 + cw_ref[0:1, lo:lo + FF_BLK] * prev2_fn(u, lo)
                 + cw_ref[1:2, lo:lo + FF_BLK] * prev1_fn(u, lo))
            up_sink(u, lo)
            halves.append(c)
        act = (_gelu(halves[0]) * halves[1]).astype(BF16)
        part = _dot(act, wdn_ref[j * FF_BLK:(j + 1) * FF_BLK, :])
        acc = part if acc is None else acc + part
    return acc


def _mix_in_kernel(x_ref, g_ref, w_ref, lng_ref, lnb_ref, wsp_ref, bsp_ref, wpa_ref,
                   qkv_ref, kvt_ref, ma_ref, sgb_ref, vn_ref, *, d_model):
    t = x_ref.shape[1]
    xn = _rmsnorm(x_ref[0], g_ref[...]).astype(BF16)
    q = _project(xn, w_ref, 0, ATT_W) * (HEAD_DIM ** -0.5)
    qkv_ref[0, :, 0:ATT_W] = q.astype(BF16)
    kv = _project(xn, w_ref, ATT_W, 3 * ATT_W)
    qkv_ref[0, :, ATT_W:3 * ATT_W] = kv.astype(BF16)
    kvt_ref[0] = kv

    u = _project(xn, w_ref, 3 * ATT_W, 3 * ATT_W + D_A)
    va = _project(xn, w_ref, 3 * ATT_W + D_A, 3 * ATT_W + 2 * D_A)
    vn = _layernorm(_gelu(va), lng_ref[...], lnb_ref[...])
    vn_ref[0] = vn[t - CHUNK:, :]
    vnb = vn.astype(BF16)
    row = lax.broadcasted_iota(jnp.int32, (CHUNK, CHUNK), 0)
    col = lax.broadcasted_iota(jnp.int32, (CHUNK, CHUNK), 1)
    w_m = [jnp.where(row >= col, wsp_ref[g], 0.0).astype(BF16) for g in range(N_GA)]
    rows = []
    for c in range(t // CHUNK):
        parts = [_dot(w_m[g], vnb[c * CHUNK:(c + 1) * CHUNK, g * GA_CH:(g + 1) * GA_CH]) + bsp_ref[g]
                 for g in range(N_GA)]
        rows.append(jnp.concatenate(parts, axis=1))
    mixed = jnp.concatenate(rows, axis=0)
    a = (_gelu(u) * mixed).astype(BF16)
    ma, sgb = _gates_and_branch_a(xn, w_ref, a, wpa_ref, d_model)
    ma_ref[0] = ma
    sgb_ref[0] = sgb


def _mix_in(x, norm_mix, w_in, ln_g, ln_b, w_sp, b_sp, w_pa, *, tile, tail):
    b, s, d = x.shape
    n_t = s // tile
    tail_blocks = tail // tile
    in_cols = w_in.shape[1]
    kern = functools.partial(_mix_in_kernel, d_model=d)
    return pl.pallas_call(
        kern,
        grid=(b, n_t),
        in_specs=[
            pl.BlockSpec((1, tile, d), lambda i, j: (i, j, 0)),
            _const_spec((1, d)),
            _const_spec((d, in_cols)),
            _const_spec((1, D_A)),
            _const_spec((1, D_A)),
            _const_spec((N_GA, CHUNK, CHUNK)),
            _const_spec((N_GA, CHUNK, 1)),
            _const_spec((D_A, d)),
        ],
        out_specs=[
            pl.BlockSpec((1, tile, 3 * ATT_W), lambda i, j: (i, j, 0)),
            pl.BlockSpec((1, tile, 2 * ATT_W), lambda i, j: (i, jnp.maximum(j - (n_t - tail_blocks), 0), 0)),
            pl.BlockSpec((1, tile, d), lambda i, j: (i, j, 0)),
            pl.BlockSpec((1, tile, d), lambda i, j: (i, j, 0)),
            pl.BlockSpec((1, CHUNK, D_A), lambda i, j: (i, 0, 0)),
        ],
        out_shape=[
            jax.ShapeDtypeStruct((b, s, 3 * ATT_W), BF16),
            jax.ShapeDtypeStruct((b, tail, 2 * ATT_W), F32),
            jax.ShapeDtypeStruct((b, s, d), F32),
            jax.ShapeDtypeStruct((b, s, d), F32),
            jax.ShapeDtypeStruct((b, CHUNK, D_A), F32),
        ],
        compiler_params=_params("arbitrary", "arbitrary"),
        name="prompt_mix_in",
    )(x, norm_mix, w_in, ln_g, ln_b, w_sp, b_sp, w_pa)


def _band_attn_kernel(cur_ref, halo_ref, o_ref):
    rows = cur_ref.shape[2]
    first = pl.program_id(2) == 0
    qi = lax.broadcasted_iota(jnp.int32, (BAND, BAND), 0)
    ki = lax.broadcasted_iota(jnp.int32, (BAND, BAND), 1)
    cur_ok = ki <= qi
    prev_ok = ki >= qi
    prev_ok_first = jnp.logical_and(prev_ok, jnp.logical_not(first))
    for qb in range(rows // BAND):
        r0 = qb * BAND
        outs, lses = [], []
        for h in range(N_HG):
            cq = slice(h * HEAD_DIM, (h + 1) * HEAD_DIM)
            ck = slice(ATT_G + h * HEAD_DIM, ATT_G + (h + 1) * HEAD_DIM)
            cv = slice(2 * ATT_G + h * HEAD_DIM, 2 * ATT_G + (h + 1) * HEAD_DIM)
            q = cur_ref[0, 0, r0:r0 + BAND, cq]
            k_c = cur_ref[0, 0, r0:r0 + BAND, ck]
            v_c = cur_ref[0, 0, r0:r0 + BAND, cv]
            if qb == 0:
                k_p, v_p, p_mask = halo_ref[0, 0, :, ck], halo_ref[0, 0, :, cv], prev_ok_first
            else:
                k_p, v_p, p_mask = cur_ref[0, 0, r0 - BAND:r0, ck], cur_ref[0, 0, r0 - BAND:r0, cv], prev_ok
            s_c = jnp.where(cur_ok, _dot_nt(q, k_c), -jnp.inf)
            s_p = jnp.where(p_mask, _dot_nt(q, k_p), -jnp.inf)
            m = jnp.maximum(jnp.max(s_c, axis=-1, keepdims=True), jnp.max(s_p, axis=-1, keepdims=True))
            p_c = jnp.exp(s_c - m)
            p_p = jnp.exp(s_p - m)
            l = jnp.sum(p_c, axis=-1, keepdims=True) + jnp.sum(p_p, axis=-1, keepdims=True)
            acc = _dot(p_c.astype(BF16), v_c) + _dot(p_p.astype(BF16), v_p)
            outs.append(acc / l)
            lses.append(jnp.broadcast_to(m + jnp.log(l), (BAND, HEAD_DIM)))
        o_ref[0, 0, r0:r0 + BAND, :] = jnp.concatenate(outs + lses, axis=1)


def _band_attn(qkv_g, *, rows):
    b, d, m, w = qkv_g.shape
    per = rows // BAND
    return pl.pallas_call(
        _band_attn_kernel,
        grid=(b, d, m // rows),
        in_specs=[
            pl.BlockSpec((1, 1, rows, w), lambda i, r, j: (i, r, j, 0)),
            pl.BlockSpec((1, 1, BAND, w), lambda i, r, j: (i, r, jnp.maximum(j * per - 1, 0), 0)),
        ],
        out_specs=pl.BlockSpec((1, 1, rows, 2 * ATT_G), lambda i, r, j: (i, r, j, 0)),
        out_shape=jax.ShapeDtypeStruct((b, d, m, 2 * ATT_G), F32),
        compiler_params=_params("arbitrary", "arbitrary", "arbitrary"),
        name=f"prompt_band_attn_d{d}",
    )(qkv_g, qkv_g)


def _merge_ffn_kernel(x_ref, ma_ref, sgb_ref, o0_ref, o1_ref, o2_ref, wpb_ref, wout_ref, nf_ref,
                      wup_ref, cw_ref, cb_ref, wdn_ref, nfin_ref, y_ref, ct_ref, carry_ref, *, d_ff):
    t = x_ref.shape[1]

    @pl.when(pl.program_id(1) == 0)
    def _():
        carry_ref[...] = jnp.zeros_like(carry_ref)

    b_out = _combine_groups([o0_ref[0], o1_ref[0], o2_ref[0]]).astype(BF16)
    merged = ma_ref[0] + sgb_ref[0] * _dot(b_out, wpb_ref[...])
    h = x_ref[0] + _dot(merged.astype(BF16), wout_ref[...])
    hn = _rmsnorm(h, nf_ref[...]).astype(BF16)
    row = lax.broadcasted_iota(jnp.int32, (t, FF_BLK), 0)

    def prev1(u, lo):
        return jnp.where(row == 0, carry_ref[7:8, lo:lo + FF_BLK], pltpu.roll(u, 1, 0))

    def prev2(u, lo):
        rolled = pltpu.roll(u, 2, 0)
        rolled = jnp.where(row == 1, carry_ref[7:8, lo:lo + FF_BLK], rolled)
        return jnp.where(row == 0, carry_ref[6:7, lo:lo + FF_BLK], rolled)

    def sink(u, lo):
        carry_ref[:, lo:lo + FF_BLK] = u[t - 8:, :]

    ffn = _conv_ffn(hn, prev1, prev2, wup_ref, cw_ref, cb_ref, wdn_ref, sink, d_ff)
    ct_ref[0] = carry_ref[8 - (CONV_W - 1):, :]
    y_ref[0] = _rmsnorm(h + ffn, nfin_ref[...])


def _merge_ffn(x, ma, sgb, o_list, w_pb, w_out, norm_ffn, w_up, conv_w, conv_b, w_dn, norm_final, *, tile):
    b, s, d = x.shape
    d_ff = w_dn.shape[0]
    tok = lambda w: pl.BlockSpec((1, tile, w), lambda i, j: (i, j, 0))
    kern = functools.partial(_merge_ffn_kernel, d_ff=d_ff)
    return pl.pallas_call(
        kern,
        grid=(b, s // tile),
        in_specs=[tok(d), tok(d), tok(d), tok(2 * ATT_G), tok(2 * ATT_G), tok(2 * ATT_G),
                  _const_spec((ATT_G, d)), _const_spec((d, d)), _const_spec((1, d)),
                  _const_spec((d, 2 * d_ff)), _const_spec((CONV_W, 2 * d_ff)), _const_spec((1, 2 * d_ff)),
                  _const_spec((d_ff, d)), _const_spec((1, d))],
        out_specs=[tok(d), pl.BlockSpec((1, CONV_W - 1, 2 * d_ff), lambda i, j: (i, 0, 0))],
        out_shape=[jax.ShapeDtypeStruct((b, s, d), F32),
                   jax.ShapeDtypeStruct((b, CONV_W - 1, 2 * d_ff), F32)],
        scratch_shapes=[pltpu.VMEM((8, 2 * d_ff), F32)],
        compiler_params=_params("arbitrary", "arbitrary"),
        name="prompt_merge_ffn",
    )(x, ma, sgb, *o_list, w_pb, w_out, norm_ffn, w_up, conv_w, conv_b, w_dn, norm_final)


def _sample_in_kernel(x_ref, g_ref, w_ref, lng_ref, lnb_ref, w00_ref, b0_ref, wpa_ref,
                      qkvt_ref, kv_ref, vn_ref, ma_ref, sgb_ref, *, d_model):
    xn = _rmsnorm(x_ref[...], g_ref[...]).astype(BF16)
    q = _project(xn, w_ref, 0, ATT_W) * (HEAD_DIM ** -0.5)
    kv = _project(xn, w_ref, ATT_W, 3 * ATT_W)
    kv_ref[...] = kv
    qkvt_ref[0:ATT_W, :] = q.T.astype(BF16)
    qkvt_ref[ATT_W:3 * ATT_W, :] = kv.T.astype(BF16)
    u = _project(xn, w_ref, 3 * ATT_W, 3 * ATT_W + D_A)
    va = _project(xn, w_ref, 3 * ATT_W + D_A, 3 * ATT_W + 2 * D_A)
    vn = _layernorm(_gelu(va), lng_ref[...], lnb_ref[...])
    vn_ref[...] = vn
    mixed = w00_ref[...] * vn + b0_ref[...]
    a = (_gelu(u) * mixed).astype(BF16)
    ma, sgb = _gates_and_branch_a(xn, w_ref, a, wpa_ref, d_model)
    ma_ref[...] = ma
    sgb_ref[...] = sgb


def _sample_in(x, norm_mix, w_in, ln_g, ln_b, w00, b0, w_pa):
    n, d = x.shape
    in_cols = w_in.shape[1]
    kern = functools.partial(_sample_in_kernel, d_model=d)
    full = lambda *shape: pl.BlockSpec(shape, lambda: (0,) * len(shape))
    return pl.pallas_call(
        kern,
        in_specs=[full(n, d), full(1, d), full(d, in_cols), full(1, D_A), full(1, D_A),
                  full(1, D_A), full(1, D_A), full(D_A, d)],
        out_specs=[full(3 * ATT_W, n), full(n, 2 * ATT_W), full(n, D_A), full(n, d), full(n, d)],
        out_shape=[jax.ShapeDtypeStruct((3 * ATT_W, n), BF16), jax.ShapeDtypeStruct((n, 2 * ATT_W), F32),
                   jax.ShapeDtypeStruct((n, D_A), F32), jax.ShapeDtypeStruct((n, d), F32),
                   jax.ShapeDtypeStruct((n, d), F32)],
        compiler_params=pltpu.CompilerParams(vmem_limit_bytes=VMEM_LIMIT_BYTES),
        name="sample_in",
    )(x, norm_mix, w_in, ln_g, ln_b, w00, b0, w_pa)


def _sample_attn_kernel(qkvt_ref, c0_ref, c1_ref, c2_ref, ot_ref, *, bb):
    step = pl.program_id(0)
    n = ot_ref.shape[1]
    lanes = 128

    @pl.when(step == 0)
    def _():
        ot_ref[...] = jnp.zeros_like(ot_ref)

    qkvt = qkvt_ref[...]
    src_row = lax.broadcasted_iota(jnp.int32, (n, lanes), 0)
    out_lane = lax.broadcasted_iota(jnp.int32, (1, n), 1)
    for bi in range(bb):
        b = step * bb + bi
        col = _dot(qkvt, (src_row == b).astype(BF16))
        per_head = []
        for h in range(N_HG):
            ms, ls, accs = [], [], []
            for g, (c_ref, (_, dil)) in enumerate(zip((c0_ref, c1_ref, c2_ref), DIL_GROUPS)):
                n_col = c_ref.shape[-1] // lanes
                r0 = g * ATT_G + h * HEAD_DIM
                q = col[r0:r0 + HEAD_DIM]
                k_new = col[ATT_W + r0:ATT_W + r0 + HEAD_DIM]
                v_new = col[2 * ATT_W + r0:2 * ATT_W + r0 + HEAD_DIM]
                s = jnp.concatenate(
                    [jnp.sum(c_ref[bi, 0, h, :, c * lanes:(c + 1) * lanes] * q, axis=0, keepdims=True)
                     for c in range(n_col)], axis=1)
                pos = lax.broadcasted_iota(jnp.int32, s.shape, 1)
                s = jnp.where(pos % dil == 0, s, -jnp.inf)
                s_new = jnp.sum(k_new * q, axis=0, keepdims=True)[:, 0:1]
                m = jnp.maximum(jnp.max(s, axis=-1, keepdims=True), s_new)
                p = jnp.exp(s - m)
                p_new = jnp.exp(s_new - m)
                acc = None
                for c in range(n_col):
                    term = c_ref[bi, 1, h, :, c * lanes:(c + 1) * lanes] * p[:, c * lanes:(c + 1) * lanes]
                    acc = term if acc is None else acc + term
                accs.append(jnp.sum(acc, axis=-1, keepdims=True) + p_new * v_new[:, 0:1])
                ls.append(jnp.sum(p, axis=-1, keepdims=True) + p_new)
                ms.append(m)
            m_all = functools.reduce(jnp.maximum, ms)
            scale = [jnp.exp(m - m_all) for m in ms]
            num = functools.reduce(lambda a, b: a + b, [sc * acc for sc, acc in zip(scale, accs)])
            den = functools.reduce(lambda a, b: a + b, [sc * l for sc, l in zip(scale, ls)])
            per_head.append(num / den)
        o_col = jnp.concatenate(per_head, axis=0)
        ot_ref[...] += o_col * (out_lane == b).astype(F32)


def _sample_attn(qkvt, caches_t, *, bb):
    n = qkvt.shape[1]
    kern = functools.partial(_sample_attn_kernel, bb=bb)
    cache_specs = [pl.BlockSpec((bb,) + c.shape[1:], lambda i: (i, 0, 0, 0, 0)) for c in caches_t]
    return pl.pallas_call(
        kern,
        grid=(n // bb,),
        in_specs=[pl.BlockSpec(qkvt.shape, lambda i: (0, 0))] + cache_specs,
        out_specs=pl.BlockSpec((ATT_G, n), lambda i: (0, 0)),
        out_shape=jax.ShapeDtypeStruct((ATT_G, n), F32),
        compiler_params=_params("arbitrary"),
        name="sample_cache_attn",
    )(qkvt, *caches_t)


def _sample_ffn_kernel(x_ref, ma_ref, sgb_ref, bt_ref, h0_ref, h1_ref, wpb_ref, wout_ref, nf_ref,
                       wup_ref, cw_ref, cb_ref, wdn_ref, nfin_ref, y_ref, up_ref, *, d_ff):
    merged = ma_ref[...] + sgb_ref[...] * _dot(bt_ref[...].T.astype(BF16), wpb_ref[...])
    h = x_ref[...] + _dot(merged.astype(BF16), wout_ref[...])
    hn = _rmsnorm(h, nf_ref[...]).astype(BF16)

    def sink(u, lo):
        up_ref[:, lo:lo + FF_BLK] = u

    ffn = _conv_ffn(hn, lambda u, lo: h1_ref[:, lo:lo + FF_BLK], lambda u, lo: h0_ref[:, lo:lo + FF_BLK],
                    wup_ref, cw_ref, cb_ref, wdn_ref, sink, d_ff)
    y_ref[...] = _rmsnorm(h + ffn, nfin_ref[...])


def _sample_ffn(x, ma, sgb, b_out, hist0, hist1, w_pb, w_out, norm_ffn, w_up, conv_w, conv_b, w_dn, norm_final):
    n, d = x.shape
    d_ff = w_dn.shape[0]
    kern = functools.partial(_sample_ffn_kernel, d_ff=d_ff)
    full = lambda *shape: pl.BlockSpec(shape, lambda: (0,) * len(shape))
    return pl.pallas_call(
        kern,
        in_specs=[full(n, d), full(n, d), full(n, d), full(ATT_G, n), full(n, 2 * d_ff), full(n, 2 * d_ff),
                  full(ATT_G, d), full(d, d), full(1, d), full(d, 2 * d_ff), full(CONV_W, 2 * d_ff),
                  full(1, 2 * d_ff), full(d_ff, d), full(1, d)],
        out_specs=[full(n, d), full(n, 2 * d_ff)],
        out_shape=[jax.ShapeDtypeStruct((n, d), F32), jax.ShapeDtypeStruct((n, 2 * d_ff), F32)],
        compiler_params=pltpu.CompilerParams(vmem_limit_bytes=VMEM_LIMIT_BYTES),
        name="sample_merge_ffn",
    )(x, ma, sgb, b_out, hist0, hist1, w_pb, w_out, norm_ffn, w_up, conv_w, conv_b, w_dn, norm_final)


def kernel(x_prompt, x_sample, cache_kv_w128, cache_kv_w512, cache_kv_w2048, state_conv_ffn, norm_mix, w_in, ln_v_gain, ln_v_bias, w_spatial, b_spatial, w_proj_a, w_proj_b, w_out, norm_ffn, w_up, conv_w, conv_b, w_down, norm_final):
    b, s, d = x_prompt.shape
    n_s, t_s, _ = x_sample.shape
    depth = w_in.shape[0]
    caches = (cache_kv_w128, cache_kv_w512, cache_kv_w2048)
    max_win = max(w for w, _ in DIL_GROUPS)
    assert depth == 1 and t_s == 1, "one layer, one new token per sample sequence"
    assert s % max_win == 0, "prompt length must cover whole strided query blocks of every dilation group"
    for (win, _), c in zip(DIL_GROUPS, caches):
        assert c.shape[2] == win, "cache must hold the whole window"

    w_in_b = w_in[0].astype(BF16)
    w_pa_b = w_proj_a[0].astype(BF16)
    w_pb_b = w_proj_b[0].astype(BF16)
    w_out_b = w_out[0].astype(BF16)
    w_up_b = w_up[0].astype(BF16)
    w_dn_b = w_down[0].astype(BF16)
    ln_g = ln_v_gain[0][None]
    ln_b = ln_v_bias[0][None]
    conv_b2 = conv_b[0][None]
    nfin = norm_final[None]

    qkv, kv_tail, ma, sgb, vn_last = _mix_in(
        x_prompt, norm_mix, w_in_b, ln_g, ln_b, w_spatial[0], b_spatial[0][:, :, None], w_pa_b,
        tile=512, tail=max_win)
    o_list = []
    for g, (win, dil) in enumerate(DIL_GROUPS):
        cols = [qkv[:, :, sec * ATT_W + g * ATT_G: sec * ATT_W + (g + 1) * ATT_G] for sec in range(3)]
        qkv_g = jnp.concatenate(cols, axis=-1).reshape(b, s // dil, dil, 3 * ATT_G).transpose(0, 2, 1, 3)
        o_g = _band_attn(qkv_g, rows=min(512, s // dil))
        o_list.append(o_g.transpose(0, 2, 1, 3).reshape(b, s, 2 * ATT_G))
    y_prompt, conv_prompt = _merge_ffn(x_prompt, ma, sgb, o_list, w_pb_b, w_out_b, norm_ffn, w_up_b,
                                       conv_w[0], conv_b2, w_dn_b, nfin, tile=256)

    kv_prompt = []
    for g, (win, _) in enumerate(DIL_GROUPS):
        k_g = kv_tail[:, max_win - win:, g * ATT_G:(g + 1) * ATT_G]
        v_g = kv_tail[:, max_win - win:, ATT_W + g * ATT_G: ATT_W + (g + 1) * ATT_G]
        kv_prompt.append(jnp.stack([k_g, v_g], axis=2).reshape(1, b, win, 2, N_HG, HEAD_DIM))

    w00 = jnp.repeat(w_spatial[0][:, 0, 0], GA_CH)[None]
    b0 = jnp.repeat(b_spatial[0][:, 0], GA_CH)[None]
    qkvt_s, kv_s, vn_s, ma_s, sgb_s = _sample_in(x_sample[:, 0], norm_mix, w_in_b, ln_g, ln_b, w00, b0, w_pa_b)
    caches_t = [c[0].transpose(0, 2, 3, 4, 1) for c in caches]
    bt_s = _sample_attn(qkvt_s, caches_t, bb=2)
    hist = state_conv_ffn[0]
    y_s, up_s = _sample_ffn(x_sample[:, 0], ma_s, sgb_s, bt_s, hist[:, 0], hist[:, 1],
                            w_pb_b, w_out_b, norm_ffn, w_up_b, conv_w[0], conv_b2, w_dn_b, nfin)

    kv_sample = [jnp.stack([kv_s[:, g * ATT_G:(g + 1) * ATT_G], kv_s[:, ATT_W + g * ATT_G: ATT_W + (g + 1) * ATT_G]],
                           axis=1).reshape(1, n_s, 1, 2, N_HG, HEAD_DIM) for g in range(N_DIL)]
    conv_sample = jnp.concatenate([hist[:, CONV_W - 2:], up_s[:, None]], axis=1)[None]

    return (y_prompt, y_s[:, None], kv_prompt[0], kv_sample[0], kv_prompt[1], kv_sample[1],
            kv_prompt[2], kv_sample[2], vn_last[None], vn_s[None, :, None], conv_prompt[None], conv_sample)
```

```python
import functools
import math

import jax
import jax.numpy as jnp
from jax import lax
from jax.experimental import pallas as pl
from jax.experimental.pallas import tpu as pltpu

F32 = jnp.float32
BF16 = jnp.bfloat16

HEAD_DIM = 64
N_HG = 4
ATT_G = N_HG * HEAD_DIM
DIL_GROUPS = ((128, 1), (512, 4), (2048, 16))
N_DIL = len(DIL_GROUPS)
ATT_W = N_DIL * ATT_G
BAND = 128
CHUNK = 128
N_GA = 4
GA_CH = 128
D_A = N_GA * GA_CH
CONV_W = 3
EPS = 1e-6
GELU_C = math.sqrt(2.0 / math.pi)

VMEM_LIMIT_BYTES = 56 * 1024 * 1024
FF_BLK = 768
LANES = 128


def _gelu(x):
    return x * (0.5 * (1.0 + jnp.tanh(GELU_C * (x + 0.044715 * (x * x * x)))))


def _rmsnorm(x, g):
    return x * lax.rsqrt(jnp.mean(x * x, axis=-1, keepdims=True) + EPS) * g


def _layernorm(x, g, b):
    mu = jnp.mean(x, axis=-1, keepdims=True)
    xc = x - mu
    var = jnp.mean(xc * xc, axis=-1, keepdims=True)
    return xc * lax.rsqrt(var + EPS) * g + b


def _dot(a, b):
    return jnp.dot(a, b, preferred_element_type=F32)


def _dot_nt(a, b):
    return lax.dot_general(a, b, (((1,), (1,)), ((), ())), preferred_element_type=F32)


def _const_spec(shape):
    nd = len(shape)
    return pl.BlockSpec(shape, lambda *_: (0,) * nd, pipeline_mode=pl.Buffered(1))


def _params(*sem):
    return pltpu.CompilerParams(dimension_semantics=sem, vmem_limit_bytes=VMEM_LIMIT_BYTES)


def _project(xn, w_ref, lo, hi):
    return _dot(xn, w_ref[:, lo:hi])


def _gates_and_branch_a(xn, w_ref, a_bf16, wpa_ref, d_model):
    base = 3 * ATT_W + 2 * D_A
    pa = _dot(a_bf16, wpa_ref[...])
    ga = _project(xn, w_ref, base, base + d_model)
    gb = _project(xn, w_ref, base + d_model, base + 2 * d_model)
    return jax.nn.sigmoid(ga) * pa, jax.nn.sigmoid(gb)


def _combine_groups(o_list):
    lses = [o[:, ATT_G:] for o in o_list]
    m = functools.reduce(jnp.maximum, lses)
    ws = [jnp.exp(l - m) for l in lses]
    den = functools.reduce(lambda a, b: a + b, ws)
    num = functools.reduce(lambda a, b: a + b, [w * o[:, :ATT_G] for w, o in zip(ws, o_list)])
    return num / den


def _conv_ffn(hn, history_fn, wup_ref, cw_ref, cb_ref, wdn_ref, d_ff):
    acc = None
    for j0 in range(0, d_ff, FF_BLK):
        wid = min(FF_BLK, d_ff - j0)
        halves = []
        for base in (0, d_ff):
            lo = base + j0
            u = _dot(hn, wup_ref[:, lo:lo + wid])
            prev1, prev2 = history_fn(u, lo)
            c = (cb_ref[:, lo:lo + wid] + cw_ref[2:3, lo:lo + wid] * u
                 + cw_ref[0:1, lo:lo + wid] * prev2
                 + cw_ref[1:2, lo:lo + wid] * prev1)
            halves.append(c)
        act = (_gelu(halves[0]) * halves[1]).astype(BF16)
        part = _dot(act, wdn_ref[j0:j0 + wid, :])
        acc = part if acc is None else acc + part
    return acc


def _mix_in_kernel(x_ref, g_ref, w_ref, lng_ref, lnb_ref, wsp_ref, bsp_ref, wpa_ref,
                   qkv_ref, kvt_ref, ma_ref, sgb_ref, vn_ref, *, d_model):
    t = x_ref.shape[1]
    xn = _rmsnorm(x_ref[0], g_ref[...]).astype(BF16)
    q = _project(xn, w_ref, 0, ATT_W) * (HEAD_DIM ** -0.5)
    qkv_ref[0, :, 0:ATT_W] = q.astype(BF16)
    kv = _project(xn, w_ref, ATT_W, 3 * ATT_W)
    qkv_ref[0, :, ATT_W:3 * ATT_W] = kv.astype(BF16)
    kvt_ref[0] = kv

    u = _project(xn, w_ref, 3 * ATT_W, 3 * ATT_W + D_A)
    va = _project(xn, w_ref, 3 * ATT_W + D_A, 3 * ATT_W + 2 * D_A)
    vn = _layernorm(_gelu(va), lng_ref[...], lnb_ref[...])
    vn_ref[0] = vn[t - CHUNK:, :]
    vnb = vn.astype(BF16)
    row = lax.broadcasted_iota(jnp.int32, (CHUNK, CHUNK), 0)
    col = lax.broadcasted_iota(jnp.int32, (CHUNK, CHUNK), 1)
    w_m = [jnp.where(row >= col, wsp_ref[g], 0.0).astype(BF16) for g in range(N_GA)]
    rows = []
    for c in range(t // CHUNK):
        parts = [_dot(w_m[g], vnb[c * CHUNK:(c + 1) * CHUNK, g * GA_CH:(g + 1) * GA_CH]) + bsp_ref[g]
                 for g in range(N_GA)]
        rows.append(jnp.concatenate(parts, axis=1))
    mixed = jnp.concatenate(rows, axis=0)
    a = (_gelu(u) * mixed).astype(BF16)
    ma, sgb = _gates_and_branch_a(xn, w_ref, a, wpa_ref, d_model)
    ma_ref[0] = ma.astype(BF16)
    sgb_ref[0] = sgb.astype(BF16)


def _mix_in(x, norm_mix, w_in, ln_g, ln_b, w_sp, b_sp, w_pa, *, tile, tail):
    b, s, d = x.shape
    n_t = s // tile
    tail_blocks = tail // tile
    in_cols = w_in.shape[1]
    kern = functools.partial(_mix_in_kernel, d_model=d)
    return pl.pallas_call(
        kern,
        grid=(b, n_t),
        in_specs=[
            pl.BlockSpec((1, tile, d), lambda i, j: (i, j, 0)),
            _const_spec((1, d)),
            _const_spec((d, in_cols)),
            _const_spec((1, D_A)),
            _const_spec((1, D_A)),
            _const_spec((N_GA, CHUNK, CHUNK)),
            _const_spec((N_GA, CHUNK, 1)),
            _const_spec((D_A, d)),
        ],
        out_specs=[
            pl.BlockSpec((1, tile, 3 * ATT_W), lambda i, j: (i, j, 0)),
            pl.BlockSpec((1, tile, 2 * ATT_W), lambda i, j: (i, jnp.maximum(j - (n_t - tail_blocks), 0), 0)),
            pl.BlockSpec((1, tile, d), lambda i, j: (i, j, 0)),
            pl.BlockSpec((1, tile, d), lambda i, j: (i, j, 0)),
            pl.BlockSpec((1, CHUNK, D_A), lambda i, j: (i, 0, 0)),
        ],
        out_shape=[
            jax.ShapeDtypeStruct((b, s, 3 * ATT_W), BF16),
            jax.ShapeDtypeStruct((b, tail, 2 * ATT_W), F32),
            jax.ShapeDtypeStruct((b, s, d), BF16),
            jax.ShapeDtypeStruct((b, s, d), BF16),
            jax.ShapeDtypeStruct((b, CHUNK, D_A), F32),
        ],
        compiler_params=_params("arbitrary", "arbitrary"),
        name="prompt_mix_in",
    )(x, norm_mix, w_in, ln_g, ln_b, w_sp, b_sp, w_pa)


def _band_attn_kernel(cur_ref, halo_ref, o_ref):
    rows = cur_ref.shape[2]
    first = pl.program_id(2) == 0
    qi = lax.broadcasted_iota(jnp.int32, (BAND, BAND), 0)
    ki = lax.broadcasted_iota(jnp.int32, (BAND, BAND), 1)
    cur_ok = ki <= qi
    prev_ok = ki >= qi
    prev_ok_first = jnp.logical_and(prev_ok, jnp.logical_not(first))
    for qb in range(rows // BAND):
        r0 = qb * BAND
        outs, lses = [], []
        for h in range(N_HG):
            cq = slice(h * HEAD_DIM, (h + 1) * HEAD_DIM)
            ck = slice(ATT_G + h * HEAD_DIM, ATT_G + (h + 1) * HEAD_DIM)
            cv = slice(2 * ATT_G + h * HEAD_DIM, 2 * ATT_G + (h + 1) * HEAD_DIM)
            q = cur_ref[0, 0, r0:r0 + BAND, cq]
            k_c = cur_ref[0, 0, r0:r0 + BAND, ck]
            v_c = cur_ref[0, 0, r0:r0 + BAND, cv]
            if qb == 0:
                k_p, v_p, p_mask = halo_ref[0, 0, :, ck], halo_ref[0, 0, :, cv], prev_ok_first
            else:
                k_p, v_p, p_mask = cur_ref[0, 0, r0 - BAND:r0, ck], cur_ref[0, 0, r0 - BAND:r0, cv], prev_ok
            s_c = jnp.where(cur_ok, _dot_nt(q, k_c), -jnp.inf)
            s_p = jnp.where(p_mask, _dot_nt(q, k_p), -jnp.inf)
            m = jnp.maximum(jnp.max(s_c, axis=-1, keepdims=True), jnp.max(s_p, axis=-1, keepdims=True))
            p_c = jnp.exp(s_c - m)
            p_p = jnp.exp(s_p - m)
            l = jnp.sum(p_c, axis=-1, keepdims=True) + jnp.sum(p_p, axis=-1, keepdims=True)
            acc = _dot(p_c.astype(BF16), v_c) + _dot(p_p.astype(BF16), v_p)
            outs.append(acc / l)
            lses.append(jnp.broadcast_to(m + jnp.log(l), (BAND, HEAD_DIM)))
        o_ref[0, 0, r0:r0 + BAND, :] = jnp.concatenate(outs + lses, axis=1)


def _band_attn(qkv_g, *, rows):
    b, d, m, w = qkv_g.shape
    per = rows // BAND
    return pl.pallas_call(
        _band_attn_kernel,
        grid=(b, d, m // rows),
        in_specs=[
            pl.BlockSpec((1, 1, rows, w), lambda i, r, j: (i, r, j, 0)),
            pl.BlockSpec((1, 1, BAND, w), lambda i, r, j: (i, r, jnp.maximum(j * per - 1, 0), 0)),
        ],
        out_specs=pl.BlockSpec((1, 1, rows, 2 * ATT_G), lambda i, r, j: (i, r, j, 0)),
        out_shape=jax.ShapeDtypeStruct((b, d, m, 2 * ATT_G), F32),
        compiler_params=_params("arbitrary", "arbitrary", "arbitrary"),
        name=f"prompt_band_attn_d{d}",
    )(qkv_g, qkv_g)


def _merge_ffn_kernel(x_ref, ma_ref, sgb_ref, o0_ref, o1_ref, o2_ref, wpb_ref, wout_ref, nf_ref,
                      wup_ref, cw_ref, cb_ref, wdn_ref, nfin_ref, y_ref, ct_ref, carry_ref, hist_ref, *, d_ff):
    t = x_ref.shape[1]

    @pl.when(pl.program_id(1) == 0)
    def _():
        carry_ref[...] = jnp.zeros_like(carry_ref)

    b_out = _combine_groups([o0_ref[0], o1_ref[0], o2_ref[0]]).astype(BF16)
    merged = ma_ref[0].astype(F32) + sgb_ref[0].astype(F32) * _dot(b_out, wpb_ref[...])
    h = x_ref[0] + _dot(merged.astype(BF16), wout_ref[...])
    hn = _rmsnorm(h, nf_ref[...]).astype(BF16)

    def history(u, lo):
        prev1, prev2 = [], []
        for k in range(u.shape[1] // LANES):
            cols = slice(lo + k * LANES, lo + (k + 1) * LANES)
            hist_ref[k, pl.ds(0, 8, stride=2), :] = carry_ref[:, cols]
            hist_ref[k, pl.ds(16, t, stride=2), :] = u[:, k * LANES:(k + 1) * LANES]
            carry_ref[:, cols] = u[t - 8:, k * LANES:(k + 1) * LANES]
            prev1.append(hist_ref[k, pl.ds(14, t, stride=2), :])
            prev2.append(hist_ref[k, pl.ds(12, t, stride=2), :])
        return jnp.concatenate(prev1, axis=1), jnp.concatenate(prev2, axis=1)

    ffn = _conv_ffn(hn, history, wup_ref, cw_ref, cb_ref, wdn_ref, d_ff)
    ct_ref[0] = carry_ref[8 - (CONV_W - 1):, :]
    y_ref[0] = _rmsnorm(h + ffn, nfin_ref[...])


def _merge_ffn(x, ma, sgb, o_list, w_pb, w_out, norm_ffn, w_up, conv_w, conv_b, w_dn, norm_final, *, tile):
    b, s, d = x.shape
    d_ff = w_dn.shape[0]
    tok = lambda w: pl.BlockSpec((1, tile, w), lambda i, j: (i, j, 0))
    kern = functools.partial(_merge_ffn_kernel, d_ff=d_ff)
    return pl.pallas_call(
        kern,
        grid=(b, s // tile),
        in_specs=[tok(d), tok(d), tok(d), tok(2 * ATT_G), tok(2 * ATT_G), tok(2 * ATT_G),
                  _const_spec((ATT_G, d)), _const_spec((d, d)), _const_spec((1, d)),
                  _const_spec((d, 2 * d_ff)), _const_spec((CONV_W, 2 * d_ff)), _const_spec((1, 2 * d_ff)),
                  _const_spec((d_ff, d)), _const_spec((1, d))],
        out_specs=[tok(d), pl.BlockSpec((1, CONV_W - 1, 2 * d_ff), lambda i, j: (i, 0, 0))],
        out_shape=[jax.ShapeDtypeStruct((b, s, d), F32),
                   jax.ShapeDtypeStruct((b, CONV_W - 1, 2 * d_ff), F32)],
        scratch_shapes=[pltpu.VMEM((8, 2 * d_ff), F32),
                        pltpu.VMEM((FF_BLK // LANES, 2 * (tile + 8), LANES), F32)],
        compiler_params=_params("arbitrary", "arbitrary"),
        name="prompt_merge_ffn",
    )(x, ma, sgb, *o_list, w_pb, w_out, norm_ffn, w_up, conv_w, conv_b, w_dn, norm_final)


def _sample_in_kernel(x_ref, g_ref, w_ref, lng_ref, lnb_ref, w00_ref, b0_ref, wpa_ref,
                      qkvt_ref, kv_ref, vn_ref, ma_ref, sgb_ref, *, d_model):
    xn = _rmsnorm(x_ref[...], g_ref[...]).astype(BF16)
    q = _project(xn, w_ref, 0, ATT_W) * (HEAD_DIM ** -0.5)
    kv = _project(xn, w_ref, ATT_W, 3 * ATT_W)
    kv_ref[...] = kv
    qkvt_ref[0:ATT_W, :] = q.T.astype(BF16)
    qkvt_ref[ATT_W:3 * ATT_W, :] = kv.T.astype(BF16)
    u = _project(xn, w_ref, 3 * ATT_W, 3 * ATT_W + D_A)
    va = _project(xn, w_ref, 3 * ATT_W + D_A, 3 * ATT_W + 2 * D_A)
    vn = _layernorm(_gelu(va), lng_ref[...], lnb_ref[...])
    vn_ref[...] = vn
    mixed = w00_ref[...] * vn + b0_ref[...]
    a = (_gelu(u) * mixed).astype(BF16)
    ma, sgb = _gates_and_branch_a(xn, w_ref, a, wpa_ref, d_model)
    ma_ref[...] = ma
    sgb_ref[...] = sgb


def _sample_in(x, norm_mix, w_in, ln_g, ln_b, w00, b0, w_pa):
    n, d = x.shape
    in_cols = w_in.shape[1]
    kern = functools.partial(_sample_in_kernel, d_model=d)
    full = lambda *shape: pl.BlockSpec(shape, lambda: (0,) * len(shape))
    return pl.pallas_call(
        kern,
        in_specs=[full(n, d), full(1, d), full(d, in_cols), full(1, D_A), full(1, D_A),
                  full(1, D_A), full(1, D_A), full(D_A, d)],
        out_specs=[full(3 * ATT_W, n), full(n, 2 * ATT_W), full(n, D_A), full(n, d), full(n, d)],
        out_shape=[jax.ShapeDtypeStruct((3 * ATT_W, n), BF16), jax.ShapeDtypeStruct((n, 2 * ATT_W), F32),
                   jax.ShapeDtypeStruct((n, D_A), F32), jax.ShapeDtypeStruct((n, d), F32),
                   jax.ShapeDtypeStruct((n, d), F32)],
        compiler_params=pltpu.CompilerParams(vmem_limit_bytes=VMEM_LIMIT_BYTES),
        name="sample_in",
    )(x, norm_mix, w_in, ln_g, ln_b, w00, b0, w_pa)


def _sample_attn_kernel(qkvt_ref, c0_ref, c1_ref, c2_ref, ot_ref, *, bb):
    step = pl.program_id(0)
    n = ot_ref.shape[1]
    lanes = 128

    @pl.when(step == 0)
    def _():
        ot_ref[...] = jnp.zeros_like(ot_ref)

    qkvt = qkvt_ref[...]
    src_row = lax.broadcasted_iota(jnp.int32, (n, lanes), 0)
    out_lane = lax.broadcasted_iota(jnp.int32, (1, n), 1)
    for bi in range(bb):
        b = step * bb + bi
        col = _dot(qkvt, (src_row == b).astype(BF16))
        per_head = []
        for h in range(N_HG):
            ms, ls, accs = [], [], []
            for g, (c_ref, (_, dil)) in enumerate(zip((c0_ref, c1_ref, c2_ref), DIL_GROUPS)):
                n_col = c_ref.shape[-1] // lanes
                r0 = g * ATT_G + h * HEAD_DIM
                q = col[r0:r0 + HEAD_DIM]
                k_new = col[ATT_W + r0:ATT_W + r0 + HEAD_DIM]
                v_new = col[2 * ATT_W + r0:2 * ATT_W + r0 + HEAD_DIM]
                s = jnp.concatenate(
                    [jnp.sum(c_ref[bi, 0, h, :, c * lanes:(c + 1) * lanes] * q, axis=0, keepdims=True)
                     for c in range(n_col)], axis=1)
                pos = lax.broadcasted_iota(jnp.int32, s.shape, 1)
                s = jnp.where(pos % dil == 0, s, -jnp.inf)
                s_new = jnp.sum(k_new * q, axis=0, keepdims=True)[:, 0:1]
                m = jnp.maximum(jnp.max(s, axis=-1, keepdims=True), s_new)
                p = jnp.exp(s - m)
                p_new = jnp.exp(s_new - m)
                acc = None
                for c in range(n_col):
                    term = c_ref[bi, 1, h, :, c * lanes:(c + 1) * lanes] * p[:, c * lanes:(c + 1) * lanes]
                    acc = term if acc is None else acc + term
                accs.append(jnp.sum(acc, axis=-1, keepdims=True) + p_new * v_new[:, 0:1])
                ls.append(jnp.sum(p, axis=-1, keepdims=True) + p_new)
                ms.append(m)
            m_all = functools.reduce(jnp.maximum, ms)
            scale = [jnp.exp(m - m_all) for m in ms]
            num = functools.reduce(lambda a, b: a + b, [sc * acc for sc, acc in zip(scale, accs)])
            den = functools.reduce(lambda a, b: a + b, [sc * l for sc, l in zip(scale, ls)])
            per_head.append(num / den)
        o_col = jnp.concatenate(per_head, axis=0)
        ot_ref[...] += o_col * (out_lane == b).astype(F32)


def _sample_attn(qkvt, caches_t, *, bb):
    n = qkvt.shape[1]
    kern = functools.partial(_sample_attn_kernel, bb=bb)
    cache_specs = [pl.BlockSpec((bb,) + c.shape[1:], lambda i: (i, 0, 0, 0, 0)) for c in caches_t]
    return pl.pallas_call(
        kern,
        grid=(n // bb,),
        in_specs=[pl.BlockSpec(qkvt.shape, lambda i: (0, 0))] + cache_specs,
        out_specs=pl.BlockSpec((ATT_G, n), lambda i: (0, 0)),
        out_shape=jax.ShapeDtypeStruct((ATT_G, n), F32),
        compiler_params=_params("arbitrary"),
        name="sample_cache_attn",
    )(qkvt, *caches_t)


def _sample_ffn_kernel(x_ref, ma_ref, sgb_ref, bt_ref, h0_ref, h1_ref, wpb_ref, wout_ref, nf_ref,
                       wup_ref, cw_ref, cb_ref, wdn_ref, nfin_ref, y_ref, up_ref, *, d_ff):
    merged = ma_ref[...] + sgb_ref[...] * _dot(bt_ref[...].T.astype(BF16), wpb_ref[...])
    h = x_ref[...] + _dot(merged.astype(BF16), wout_ref[...])
    hn = _rmsnorm(h, nf_ref[...]).astype(BF16)

    def history(u, lo):
        up_ref[:, lo:lo + u.shape[1]] = u
        return h1_ref[:, lo:lo + u.shape[1]], h0_ref[:, lo:lo + u.shape[1]]

    ffn = _conv_ffn(hn, history, wup_ref, cw_ref, cb_ref, wdn_ref, d_ff)
    y_ref[...] = _rmsnorm(h + ffn, nfin_ref[...])


def _sample_ffn(x, ma, sgb, b_out, hist0, hist1, w_pb, w_out, norm_ffn, w_up, conv_w, conv_b, w_dn, norm_final):
    n, d = x.shape
    d_ff = w_dn.shape[0]
    kern = functools.partial(_sample_ffn_kernel, d_ff=d_ff)
    full = lambda *shape: pl.BlockSpec(shape, lambda: (0,) * len(shape))
    return pl.pallas_call(
        kern,
        in_specs=[full(n, d), full(n, d), full(n, d), full(ATT_G, n), full(n, 2 * d_ff), full(n, 2 * d_ff),
                  full(ATT_G, d), full(d, d), full(1, d), full(d, 2 * d_ff), full(CONV_W, 2 * d_ff),
                  full(1, 2 * d_ff), full(d_ff, d), full(1, d)],
        out_specs=[full(n, d), full(n, 2 * d_ff)],
        out_shape=[jax.ShapeDtypeStruct((n, d), F32), jax.ShapeDtypeStruct((n, 2 * d_ff), F32)],
        compiler_params=pltpu.CompilerParams(vmem_limit_bytes=VMEM_LIMIT_BYTES),
        name="sample_merge_ffn",
    )(x, ma, sgb, b_out, hist0, hist1, w_pb, w_out, norm_ffn, w_up, conv_w, conv_b, w_dn, norm_final)


def kernel(x_prompt, x_sample, cache_kv_w128, cache_kv_w512, cache_kv_w2048, state_conv_ffn, norm_mix, w_in, ln_v_gain, ln_v_bias, w_spatial, b_spatial, w_proj_a, w_proj_b, w_out, norm_ffn, w_up, conv_w, conv_b, w_down, norm_final):
    b, s, d = x_prompt.shape
    n_s, t_s, _ = x_sample.shape
    depth = w_in.shape[0]
    caches = (cache_kv_w128, cache_kv_w512, cache_kv_w2048)
    max_win = max(w for w, _ in DIL_GROUPS)
    assert depth == 1 and t_s == 1, "one layer, one new token per sample sequence"
    assert s % max_win == 0, "prompt length must cover whole strided query blocks of every dilation group"
    for (win, _), c in zip(DIL_GROUPS, caches):
        assert c.shape[2] == win, "cache must hold the whole window"

    w_in_b = w_in[0].astype(BF16)
    w_pa_b = w_proj_a[0].astype(BF16)
    w_pb_b = w_proj_b[0].astype(BF16)
    w_out_b = w_out[0].astype(BF16)
    w_up_b = w_up[0].astype(BF16)
    w_dn_b = w_down[0].astype(BF16)
    ln_g = ln_v_gain[0][None]
    ln_b = ln_v_bias[0][None]
    conv_b2 = conv_b[0][None]
    nfin = norm_final[None]

    qkv, kv_tail, ma, sgb, vn_last = _mix_in(
        x_prompt, norm_mix, w_in_b, ln_g, ln_b, w_spatial[0], b_spatial[0][:, :, None], w_pa_b,
        tile=512, tail=max_win)
    o_list = []
    for g, (win, dil) in enumerate(DIL_GROUPS):
        cols = [qkv[:, :, sec * ATT_W + g * ATT_G: sec * ATT_W + (g + 1) * ATT_G] for sec in range(3)]
        qkv_g = jnp.concatenate(cols, axis=-1).reshape(b, s // dil, dil, 3 * ATT_G).transpose(0, 2, 1, 3)
        o_g = _band_attn(qkv_g, rows=min(512, s // dil))
        o_list.append(o_g.transpose(0, 2, 1, 3).reshape(b, s, 2 * ATT_G))
    y_prompt, conv_prompt = _merge_ffn(x_prompt, ma, sgb, o_list, w_pb_b, w_out_b, norm_ffn, w_up_b,
                                       conv_w[0], conv_b2, w_dn_b, nfin, tile=512)

    kv_prompt = []
    for g, (win, _) in enumerate(DIL_GROUPS):
        k_g = kv_tail[:, max_win - win:, g * ATT_G:(g + 1) * ATT_G]
        v_g = kv_tail[:, max_win - win:, ATT_W + g * ATT_G: ATT_W + (g + 1) * ATT_G]
        kv_prompt.append(jnp.stack([k_g, v_g], axis=2).reshape(1, b, win, 2, N_HG, HEAD_DIM))

    w00 = jnp.repeat(w_spatial[0][:, 0, 0], GA_CH)[None]
    b0 = jnp.repeat(b_spatial[0][:, 0], GA_CH)[None]
    qkvt_s, kv_s, vn_s, ma_s, sgb_s = _sample_in(x_sample[:, 0], norm_mix, w_in_b, ln_g, ln_b, w00, b0, w_pa_b)
    caches_t = [c[0].transpose(0, 2, 3, 4, 1) for c in caches]
    bt_s = _sample_attn(qkvt_s, caches_t, bb=2)
    hist = state_conv_ffn[0]
    y_s, up_s = _sample_ffn(x_sample[:, 0], ma_s, sgb_s, bt_s, hist[:, 0], hist[:, 1],
                            w_pb_b, w_out_b, norm_ffn, w_up_b, conv_w[0], conv_b2, w_dn_b, nfin)

    kv_sample = [jnp.stack([kv_s[:, g * ATT_G:(g + 1) * ATT_G], kv_s[:, ATT_W + g * ATT_G: ATT_W + (g + 1) * ATT_G]],
                           axis=1).reshape(1, n_s, 1, 2, N_HG, HEAD_DIM) for g in range(N_DIL)]
    conv_sample = jnp.concatenate([hist[:, CONV_W - 2:], up_s[:, None]], axis=1)[None]

    return (y_prompt, y_s[:, None], kv_prompt[0], kv_sample[0], kv_prompt[1], kv_sample[1],
            kv_prompt[2], kv_sample[2], vn_last[None], vn_s[None, :, None], conv_prompt[None], conv_sample)
```

```python
import functools
import math

import jax
import jax.numpy as jnp
from jax import lax
from jax.experimental import pallas as pl
from jax.experimental.pallas import tpu as pltpu

F32 = jnp.float32
BF16 = jnp.bfloat16

HEAD_DIM = 64
N_HG = 4
ATT_G = N_HG * HEAD_DIM
DIL_GROUPS = ((128, 1), (512, 4), (2048, 16))
N_DIL = len(DIL_GROUPS)
ATT_W = N_DIL * ATT_G
BAND = 128
CHUNK = 128
N_GA = 4
GA_CH = 128
D_A = N_GA * GA_CH
CONV_W = 3
EPS = 1e-6
GELU_C = math.sqrt(2.0 / math.pi)

VMEM_LIMIT_BYTES = 56 * 1024 * 1024
FF_BLK = 768
LANES = 128


def _gelu(x):
    return x * (0.5 * (1.0 + jnp.tanh(GELU_C * (x + 0.044715 * (x * x * x)))))


def _rmsnorm(x, g):
    return x * lax.rsqrt(jnp.mean(x * x, axis=-1, keepdims=True) + EPS) * g


def _layernorm(x, g, b):
    mu = jnp.mean(x, axis=-1, keepdims=True)
    xc = x - mu
    var = jnp.mean(xc * xc, axis=-1, keepdims=True)
    return xc * lax.rsqrt(var + EPS) * g + b


def _dot(a, b):
    return jnp.dot(a, b, preferred_element_type=F32)


def _dot_nt(a, b):
    return lax.dot_general(a, b, (((1,), (1,)), ((), ())), preferred_element_type=F32)


def _const_spec(shape):
    nd = len(shape)
    return pl.BlockSpec(shape, lambda *_: (0,) * nd, pipeline_mode=pl.Buffered(1))


def _params(*sem):
    return pltpu.CompilerParams(dimension_semantics=sem, vmem_limit_bytes=VMEM_LIMIT_BYTES)


def _project(xn, w_ref, lo, hi):
    return _dot(xn, w_ref[:, lo:hi])


def _gates_and_branch_a(xn, w_ref, a_bf16, wpa_ref, d_model):
    base = 3 * ATT_W + 2 * D_A
    pa = _dot(a_bf16, wpa_ref[...])
    ga = _project(xn, w_ref, base, base + d_model)
    gb = _project(xn, w_ref, base + d_model, base + 2 * d_model)
    return jax.nn.sigmoid(ga) * pa, jax.nn.sigmoid(gb)


def _combine_groups(o_list):
    lses = [o[:, ATT_G:] for o in o_list]
    m = functools.reduce(jnp.maximum, lses)
    ws = [jnp.exp(l - m) for l in lses]
    den = functools.reduce(lambda a, b: a + b, ws)
    num = functools.reduce(lambda a, b: a + b, [w * o[:, :ATT_G] for w, o in zip(ws, o_list)])
    return num / den


def _conv_ffn(hn, history_fn, wup_ref, cw_ref, cb_ref, wdn_ref, d_ff):
    acc = None
    for j0 in range(0, d_ff, FF_BLK):
        wid = min(FF_BLK, d_ff - j0)
        halves = []
        for base in (0, d_ff):
            lo = base + j0
            u = _dot(hn, wup_ref[:, lo:lo + wid])
            prev1, prev2 = history_fn(u, lo)
            c = (cb_ref[:, lo:lo + wid] + cw_ref[2:3, lo:lo + wid] * u
                 + cw_ref[0:1, lo:lo + wid] * prev2
                 + cw_ref[1:2, lo:lo + wid] * prev1)
            halves.append(c)
        act = (_gelu(halves[0]) * halves[1]).astype(BF16)
        part = _dot(act, wdn_ref[j0:j0 + wid, :])
        acc = part if acc is None else acc + part
    return acc


def _mix_in_kernel(x_ref, g_ref, w_ref, lng_ref, lnb_ref, wsp_ref, bsp_ref, wpa_ref,
                   qkv0_ref, qkv1_ref, qkv2_ref, kvt_ref, ma_ref, sgb_ref, vn_ref, slab_ref, *, d_model):
    t = x_ref.shape[1]
    xn = _rmsnorm(x_ref[0], g_ref[...]).astype(BF16)
    q = _project(xn, w_ref, 0, ATT_W) * (HEAD_DIM ** -0.5)
    kv = _project(xn, w_ref, ATT_W, 3 * ATT_W)
    kvt_ref[0] = kv
    per_slab = ATT_G // LANES
    for g, ((_, dil), out_ref) in enumerate(zip(DIL_GROUPS, (qkv0_ref, qkv1_ref, qkv2_ref))):
        secs = [q[:, g * ATT_G:(g + 1) * ATT_G], kv[:, g * ATT_G:(g + 1) * ATT_G],
                kv[:, ATT_W + g * ATT_G:ATT_W + (g + 1) * ATT_G]]
        if dil == 1:
            out_ref[0, 0] = jnp.concatenate(secs, axis=1).astype(BF16)
            continue
        for i, sec in enumerate(secs):
            for k in range(per_slab):
                slab_ref[i * per_slab + k] = sec[:, k * LANES:(k + 1) * LANES]
        for r in range(dil):
            pieces = [slab_ref[n, pl.ds(r, t // dil, stride=dil), :] for n in range(3 * per_slab)]
            out_ref[0, r] = jnp.concatenate(pieces, axis=1).astype(BF16)

    u = _project(xn, w_ref, 3 * ATT_W, 3 * ATT_W + D_A)
    va = _project(xn, w_ref, 3 * ATT_W + D_A, 3 * ATT_W + 2 * D_A)
    vn = _layernorm(_gelu(va), lng_ref[...], lnb_ref[...])
    vn_ref[0] = vn[t - CHUNK:, :]
    vnb = vn.astype(BF16)
    row = lax.broadcasted_iota(jnp.int32, (CHUNK, CHUNK), 0)
    col = lax.broadcasted_iota(jnp.int32, (CHUNK, CHUNK), 1)
    w_m = [jnp.where(row >= col, wsp_ref[g], 0.0).astype(BF16) for g in range(N_GA)]
    rows = []
    for c in range(t // CHUNK):
        parts = [_dot(w_m[g], vnb[c * CHUNK:(c + 1) * CHUNK, g * GA_CH:(g + 1) * GA_CH]) + bsp_ref[g]
                 for g in range(N_GA)]
        rows.append(jnp.concatenate(parts, axis=1))
    mixed = jnp.concatenate(rows, axis=0)
    a = (_gelu(u) * mixed).astype(BF16)
    ma, sgb = _gates_and_branch_a(xn, w_ref, a, wpa_ref, d_model)
    ma_ref[0] = ma.astype(BF16)
    sgb_ref[0] = sgb.astype(BF16)


def _mix_in(x, norm_mix, w_in, ln_g, ln_b, w_sp, b_sp, w_pa, *, tile, tail):
    b, s, d = x.shape
    n_t = s // tile
    tail_blocks = tail // tile
    in_cols = w_in.shape[1]
    kern = functools.partial(_mix_in_kernel, d_model=d)
    return pl.pallas_call(
        kern,
        grid=(b, n_t),
        in_specs=[
            pl.BlockSpec((1, tile, d), lambda i, j: (i, j, 0)),
            _const_spec((1, d)),
            _const_spec((d, in_cols)),
            _const_spec((1, D_A)),
            _const_spec((1, D_A)),
            _const_spec((N_GA, CHUNK, CHUNK)),
            _const_spec((N_GA, CHUNK, 1)),
            _const_spec((D_A, d)),
        ],
        out_specs=[
            pl.BlockSpec((1, dil, tile // dil, 3 * ATT_G), lambda i, j: (i, 0, j, 0)) for _, dil in DIL_GROUPS
        ] + [
            pl.BlockSpec((1, tile, 2 * ATT_W), lambda i, j: (i, jnp.maximum(j - (n_t - tail_blocks), 0), 0)),
            pl.BlockSpec((1, tile, d), lambda i, j: (i, j, 0)),
            pl.BlockSpec((1, tile, d), lambda i, j: (i, j, 0)),
            pl.BlockSpec((1, CHUNK, D_A), lambda i, j: (i, 0, 0)),
        ],
        out_shape=[
            jax.ShapeDtypeStruct((b, dil, s // dil, 3 * ATT_G), BF16) for _, dil in DIL_GROUPS
        ] + [
            jax.ShapeDtypeStruct((b, tail, 2 * ATT_W), F32),
            jax.ShapeDtypeStruct((b, s, d), BF16),
            jax.ShapeDtypeStruct((b, s, d), BF16),
            jax.ShapeDtypeStruct((b, CHUNK, D_A), F32),
        ],
        scratch_shapes=[pltpu.VMEM((3 * ATT_G // LANES, tile, LANES), F32)],
        compiler_params=_params("arbitrary", "arbitrary"),
        name="prompt_mix_in",
    )(x, norm_mix, w_in, ln_g, ln_b, w_sp, b_sp, w_pa)


def _band_attn_kernel(cur_ref, halo_ref, o_ref):
    rows = cur_ref.shape[2]
    first = pl.program_id(2) == 0
    cur = cur_ref[0, 0]
    halo = halo_ref[0, 0]
    q_all = cur[:, 0:ATT_G]
    k_all = jnp.concatenate([halo[:, ATT_G:2 * ATT_G], cur[:, ATT_G:2 * ATT_G]], axis=0)
    v_all = jnp.concatenate([halo[:, 2 * ATT_G:], cur[:, 2 * ATT_G:]], axis=0)
    lane_head = lax.broadcasted_iota(jnp.int32, (1, ATT_G), 1) // HEAD_DIM
    zero = jnp.zeros((), BF16)
    k_h = [jnp.where(lane_head == h, k_all, zero) for h in range(N_HG)]
    v_h = [jnp.where(lane_head == h, v_all, zero) for h in range(N_HG)]
    qi = lax.broadcasted_iota(jnp.int32, (BAND, 2 * BAND), 0)
    ki = lax.broadcasted_iota(jnp.int32, (BAND, 2 * BAND), 1)
    band = jnp.logical_and(ki >= qi, ki <= qi + BAND)
    band_first = jnp.logical_and(band, jnp.logical_or(ki >= BAND, jnp.logical_not(first)))
    n_qb = rows // BAND
    tiles = []
    for qb in range(n_qb):
        q = q_all[qb * BAND:(qb + 1) * BAND]
        mask = band_first if qb == 0 else band
        for h in range(N_HG):
            s = _dot_nt(q, k_h[h][qb * BAND:(qb + 2) * BAND])
            tiles.append(jnp.where(mask, s, -jnp.inf))
    s_all = jnp.concatenate(tiles, axis=0)
    m = jnp.max(s_all, axis=-1, keepdims=True)
    p = jnp.exp(s_all - m)
    l = jnp.sum(p, axis=-1, keepdims=True)
    p_n = (p * (1.0 / l)).astype(BF16)
    lse = m + jnp.log(l)
    for qb in range(n_qb):
        o, lse_row = None, None
        for h in range(N_HG):
            u0 = (qb * N_HG + h) * BAND
            part = _dot(p_n[u0:u0 + BAND], v_h[h][qb * BAND:(qb + 2) * BAND])
            lse_part = jnp.where(lane_head == h, lse[u0:u0 + BAND], 0.0)
            o = part if o is None else o + part
            lse_row = lse_part if lse_row is None else lse_row + lse_part
        o_ref[0, 0, qb * BAND:(qb + 1) * BAND, :] = jnp.concatenate([o, lse_row], axis=1)


def _band_attn(qkv_g, *, rows):
    b, d, m, w = qkv_g.shape
    per = rows // BAND
    return pl.pallas_call(
        _band_attn_kernel,
        grid=(b, d, m // rows),
        in_specs=[
            pl.BlockSpec((1, 1, rows, w), lambda i, r, j: (i, r, j, 0)),
            pl.BlockSpec((1, 1, BAND, w), lambda i, r, j: (i, r, jnp.maximum(j * per - 1, 0), 0)),
        ],
        out_specs=pl.BlockSpec((1, 1, rows, 2 * ATT_G), lambda i, r, j: (i, r, j, 0)),
        out_shape=jax.ShapeDtypeStruct((b, d, m, 2 * ATT_G), F32),
        compiler_params=_params("arbitrary", "arbitrary", "arbitrary"),
        name=f"prompt_band_attn_d{d}",
    )(qkv_g, qkv_g)


def _merge_ffn_kernel(x_ref, ma_ref, sgb_ref, o0_ref, o1_ref, o2_ref, wpb_ref, wout_ref, nf_ref,
                      wup_ref, cw_ref, cb_ref, wdn_ref, nfin_ref, y_ref, ct_ref, carry_ref, hist_ref, tok_ref,
                      *, d_ff):
    t = x_ref.shape[1]

    @pl.when(pl.program_id(1) == 0)
    def _():
        carry_ref[...] = jnp.zeros_like(carry_ref)

    n_slab = 2 * ATT_G // LANES
    o_list = []
    for (_, dil), o_ref in zip(DIL_GROUPS, (o0_ref, o1_ref, o2_ref)):
        if dil == 1:
            o_list.append(o_ref[0, 0])
            continue
        for r in range(dil):
            for k in range(n_slab):
                tok_ref[k, pl.ds(r, t // dil, stride=dil), :] = o_ref[0, r, :, k * LANES:(k + 1) * LANES]
        o_list.append(jnp.concatenate([tok_ref[k] for k in range(n_slab)], axis=1))
    b_out = _combine_groups(o_list).astype(BF16)
    merged = ma_ref[0].astype(F32) + sgb_ref[0].astype(F32) * _dot(b_out, wpb_ref[...])
    h = x_ref[0] + _dot(merged.astype(BF16), wout_ref[...])
    hn = _rmsnorm(h, nf_ref[...]).astype(BF16)

    def history(u, lo):
        prev1, prev2 = [], []
        for k in range(u.shape[1] // LANES):
            cols = slice(lo + k * LANES, lo + (k + 1) * LANES)
            hist_ref[k, pl.ds(0, 8, stride=2), :] = carry_ref[:, cols]
            hist_ref[k, pl.ds(16, t, stride=2), :] = u[:, k * LANES:(k + 1) * LANES]
            carry_ref[:, cols] = u[t - 8:, k * LANES:(k + 1) * LANES]
            prev1.append(hist_ref[k, pl.ds(14, t, stride=2), :])
            prev2.append(hist_ref[k, pl.ds(12, t, stride=2), :])
        return jnp.concatenate(prev1, axis=1), jnp.concatenate(prev2, axis=1)

    ffn = _conv_ffn(hn, history, wup_ref, cw_ref, cb_ref, wdn_ref, d_ff)
    ct_ref[0] = carry_ref[8 - (CONV_W - 1):, :]
    y_ref[0] = _rmsnorm(h + ffn, nfin_ref[...])


def _merge_ffn(x, ma, sgb, o_list, w_pb, w_out, norm_ffn, w_up, conv_w, conv_b, w_dn, norm_final, *, tile):
    b, s, d = x.shape
    d_ff = w_dn.shape[0]
    tok = lambda w: pl.BlockSpec((1, tile, w), lambda i, j: (i, j, 0))
    kern = functools.partial(_merge_ffn_kernel, d_ff=d_ff)
    return pl.pallas_call(
        kern,
        grid=(b, s // tile),
        in_specs=[tok(d), tok(d), tok(d)] + [
                  pl.BlockSpec((1, dil, tile // dil, 2 * ATT_G), lambda i, j: (i, 0, j, 0)) for _, dil in DIL_GROUPS
                  ] + [
                  _const_spec((ATT_G, d)), _const_spec((d, d)), _const_spec((1, d)),
                  _const_spec((d, 2 * d_ff)), _const_spec((CONV_W, 2 * d_ff)), _const_spec((1, 2 * d_ff)),
                  _const_spec((d_ff, d)), _const_spec((1, d))],
        out_specs=[tok(d), pl.BlockSpec((1, CONV_W - 1, 2 * d_ff), lambda i, j: (i, 0, 0))],
        out_shape=[jax.ShapeDtypeStruct((b, s, d), F32),
                   jax.ShapeDtypeStruct((b, CONV_W - 1, 2 * d_ff), F32)],
        scratch_shapes=[pltpu.VMEM((8, 2 * d_ff), F32),
                        pltpu.VMEM((FF_BLK // LANES, 2 * (tile + 8), LANES), F32),
                        pltpu.VMEM((2 * ATT_G // LANES, tile, LANES), F32)],
        compiler_params=_params("arbitrary", "arbitrary"),
        name="prompt_merge_ffn",
    )(x, ma, sgb, *o_list, w_pb, w_out, norm_ffn, w_up, conv_w, conv_b, w_dn, norm_final)


def _sample_in_kernel(x_ref, g_ref, w_ref, lng_ref, lnb_ref, w00_ref, b0_ref, wpa_ref,
                      qkvt_ref, kv_ref, vn_ref, ma_ref, sgb_ref, *, d_model):
    xn = _rmsnorm(x_ref[...], g_ref[...]).astype(BF16)
    q = _project(xn, w_ref, 0, ATT_W) * (HEAD_DIM ** -0.5)
    kv = _project(xn, w_ref, ATT_W, 3 * ATT_W)
    kv_ref[...] = kv
    qkvt_ref[0:ATT_W, :] = q.T.astype(BF16)
    qkvt_ref[ATT_W:3 * ATT_W, :] = kv.T.astype(BF16)
    u = _project(xn, w_ref, 3 * ATT_W, 3 * ATT_W + D_A)
    va = _project(xn, w_ref, 3 * ATT_W + D_A, 3 * ATT_W + 2 * D_A)
    vn = _layernorm(_gelu(va), lng_ref[...], lnb_ref[...])
    vn_ref[...] = vn
    mixed = w00_ref[...] * vn + b0_ref[...]
    a = (_gelu(u) * mixed).astype(BF16)
    ma, sgb = _gates_and_branch_a(xn, w_ref, a, wpa_ref, d_model)
    ma_ref[...] = ma
    sgb_ref[...] = sgb


def _sample_in(x, norm_mix, w_in, ln_g, ln_b, w00, b0, w_pa):
    n, d = x.shape
    in_cols = w_in.shape[1]
    kern = functools.partial(_sample_in_kernel, d_model=d)
    full = lambda *shape: pl.BlockSpec(shape, lambda: (0,) * len(shape))
    return pl.pallas_call(
        kern,
        in_specs=[full(n, d), full(1, d), full(d, in_cols), full(1, D_A), full(1, D_A),
                  full(1, D_A), full(1, D_A), full(D_A, d)],
        out_specs=[full(3 * ATT_W, n), full(n, 2 * ATT_W), full(n, D_A), full(n, d), full(n, d)],
        out_shape=[jax.ShapeDtypeStruct((3 * ATT_W, n), BF16), jax.ShapeDtypeStruct((n, 2 * ATT_W), F32),
                   jax.ShapeDtypeStruct((n, D_A), F32), jax.ShapeDtypeStruct((n, d), F32),
                   jax.ShapeDtypeStruct((n, d), F32)],
        compiler_params=pltpu.CompilerParams(vmem_limit_bytes=VMEM_LIMIT_BYTES),
        name="sample_in",
    )(x, norm_mix, w_in, ln_g, ln_b, w00, b0, w_pa)


def _sample_attn_kernel(qkvt_ref, c0_ref, c1_ref, c2_ref, ot_ref, *, bb):
    step = pl.program_id(0)
    n = ot_ref.shape[1]
    lanes = 128

    @pl.when(step == 0)
    def _():
        ot_ref[...] = jnp.zeros_like(ot_ref)

    qkvt = qkvt_ref[...]
    src_row = lax.broadcasted_iota(jnp.int32, (n, lanes), 0)
    out_lane = lax.broadcasted_iota(jnp.int32, (1, n), 1)
    for bi in range(bb):
        b = step * bb + bi
        col = _dot(qkvt, (src_row == b).astype(BF16))
        per_head = []
        for h in range(N_HG):
            ms, ls, accs = [], [], []
            for g, (c_ref, (_, dil)) in enumerate(zip((c0_ref, c1_ref, c2_ref), DIL_GROUPS)):
                n_col = c_ref.shape[-1] // lanes
                r0 = g * ATT_G + h * HEAD_DIM
                q = col[r0:r0 + HEAD_DIM]
                k_new = col[ATT_W + r0:ATT_W + r0 + HEAD_DIM]
                v_new = col[2 * ATT_W + r0:2 * ATT_W + r0 + HEAD_DIM]
                s = jnp.concatenate(
                    [jnp.sum(c_ref[bi, 0, h, :, c * lanes:(c + 1) * lanes] * q, axis=0, keepdims=True)
                     for c in range(n_col)], axis=1)
                pos = lax.broadcasted_iota(jnp.int32, s.shape, 1)
                s = jnp.where(pos % dil == 0, s, -jnp.inf)
                s_new = jnp.sum(k_new * q, axis=0, keepdims=True)[:, 0:1]
                m = jnp.maximum(jnp.max(s, axis=-1, keepdims=True), s_new)
                p = jnp.exp(s - m)
                p_new = jnp.exp(s_new - m)
                acc = None
                for c in range(n_col):
                    term = c_ref[bi, 1, h, :, c * lanes:(c + 1) * lanes] * p[:, c * lanes:(c + 1) * lanes]
                    acc = term if acc is None else acc + term
                accs.append(jnp.sum(acc, axis=-1, keepdims=True) + p_new * v_new[:, 0:1])
                ls.append(jnp.sum(p, axis=-1, keepdims=True) + p_new)
                ms.append(m)
            m_all = functools.reduce(jnp.maximum, ms)
            scale = [jnp.exp(m - m_all) for m in ms]
            num = functools.reduce(lambda a, b: a + b, [sc * acc for sc, acc in zip(scale, accs)])
            den = functools.reduce(lambda a, b: a + b, [sc * l for sc, l in zip(scale, ls)])
            per_head.append(num / den)
        o_col = jnp.concatenate(per_head, axis=0)
        ot_ref[...] += o_col * (out_lane == b).astype(F32)


def _sample_attn(qkvt, caches_t, *, bb):
    n = qkvt.shape[1]
    kern = functools.partial(_sample_attn_kernel, bb=bb)
    cache_specs = [pl.BlockSpec((bb,) + c.shape[1:], lambda i: (i, 0, 0, 0, 0)) for c in caches_t]
    return pl.pallas_call(
        kern,
        grid=(n // bb,),
        in_specs=[pl.BlockSpec(qkvt.shape, lambda i: (0, 0))] + cache_specs,
        out_specs=pl.BlockSpec((ATT_G, n), lambda i: (0, 0)),
        out_shape=jax.ShapeDtypeStruct((ATT_G, n), F32),
        compiler_params=_params("arbitrary"),
        name="sample_cache_attn",
    )(qkvt, *caches_t)


def _sample_ffn_kernel(x_ref, ma_ref, sgb_ref, bt_ref, h0_ref, h1_ref, wpb_ref, wout_ref, nf_ref,
                       wup_ref, cw_ref, cb_ref, wdn_ref, nfin_ref, y_ref, up_ref, *, d_ff):
    merged = ma_ref[...] + sgb_ref[...] * _dot(bt_ref[...].T.astype(BF16), wpb_ref[...])
    h = x_ref[...] + _dot(merged.astype(BF16), wout_ref[...])
    hn = _rmsnorm(h, nf_ref[...]).astype(BF16)

    def history(u, lo):
        up_ref[:, lo:lo + u.shape[1]] = u
        return h1_ref[:, lo:lo + u.shape[1]], h0_ref[:, lo:lo + u.shape[1]]

    ffn = _conv_ffn(hn, history, wup_ref, cw_ref, cb_ref, wdn_ref, d_ff)
    y_ref[...] = _rmsnorm(h + ffn, nfin_ref[...])


def _sample_ffn(x, ma, sgb, b_out, hist0, hist1, w_pb, w_out, norm_ffn, w_up, conv_w, conv_b, w_dn, norm_final):
    n, d = x.shape
    d_ff = w_dn.shape[0]
    kern = functools.partial(_sample_ffn_kernel, d_ff=d_ff)
    full = lambda *shape: pl.BlockSpec(shape, lambda: (0,) * len(shape))
    return pl.pallas_call(
        kern,
        in_specs=[full(n, d), full(n, d), full(n, d), full(ATT_G, n), full(n, 2 * d_ff), full(n, 2 * d_ff),
                  full(ATT_G, d), full(d, d), full(1, d), full(d, 2 * d_ff), full(CONV_W, 2 * d_ff),
                  full(1, 2 * d_ff), full(d_ff, d), full(1, d)],
        out_specs=[full(n, d), full(n, 2 * d_ff)],
        out_shape=[jax.ShapeDtypeStruct((n, d), F32), jax.ShapeDtypeStruct((n, 2 * d_ff), F32)],
        compiler_params=pltpu.CompilerParams(vmem_limit_bytes=VMEM_LIMIT_BYTES),
        name="sample_merge_ffn",
    )(x, ma, sgb, b_out, hist0, hist1, w_pb, w_out, norm_ffn, w_up, conv_w, conv_b, w_dn, norm_final)


def kernel(x_prompt, x_sample, cache_kv_w128, cache_kv_w512, cache_kv_w2048, state_conv_ffn, norm_mix, w_in, ln_v_gain, ln_v_bias, w_spatial, b_spatial, w_proj_a, w_proj_b, w_out, norm_ffn, w_up, conv_w, conv_b, w_down, norm_final):
    b, s, d = x_prompt.shape
    n_s, t_s, _ = x_sample.shape
    depth = w_in.shape[0]
    caches = (cache_kv_w128, cache_kv_w512, cache_kv_w2048)
    max_win = max(w for w, _ in DIL_GROUPS)
    assert depth == 1 and t_s == 1, "one layer, one new token per sample sequence"
    assert s % max_win == 0, "prompt length must cover whole strided query blocks of every dilation group"
    for (win, _), c in zip(DIL_GROUPS, caches):
        assert c.shape[2] == win, "cache must hold the whole window"

    w_in_b = w_in[0].astype(BF16)
    w_pa_b = w_proj_a[0].astype(BF16)
    w_pb_b = w_proj_b[0].astype(BF16)
    w_out_b = w_out[0].astype(BF16)
    w_up_b = w_up[0].astype(BF16)
    w_dn_b = w_down[0].astype(BF16)
    ln_g = ln_v_gain[0][None]
    ln_b = ln_v_bias[0][None]
    conv_b2 = conv_b[0][None]
    nfin = norm_final[None]

    *qkv_groups, kv_tail, ma, sgb, vn_last = _mix_in(
        x_prompt, norm_mix, w_in_b, ln_g, ln_b, w_spatial[0], b_spatial[0][:, :, None], w_pa_b,
        tile=512, tail=max_win)
    o_list = [_band_attn(qkv_g, rows=min(512, s // dil)) for qkv_g, (_, dil) in zip(qkv_groups, DIL_GROUPS)]
    y_prompt, conv_prompt = _merge_ffn(x_prompt, ma, sgb, o_list, w_pb_b, w_out_b, norm_ffn, w_up_b,
                                       conv_w[0], conv_b2, w_dn_b, nfin, tile=512)

    kv_prompt = []
    for g, (win, _) in enumerate(DIL_GROUPS):
        k_g = kv_tail[:, max_win - win:, g * ATT_G:(g + 1) * ATT_G]
        v_g = kv_tail[:, max_win - win:, ATT_W + g * ATT_G: ATT_W + (g + 1) * ATT_G]
        kv_prompt.append(jnp.stack([k_g, v_g], axis=2).reshape(1, b, win, 2, N_HG, HEAD_DIM))

    w00 = jnp.repeat(w_spatial[0][:, 0, 0], GA_CH)[None]
    b0 = jnp.repeat(b_spatial[0][:, 0], GA_CH)[None]
    qkvt_s, kv_s, vn_s, ma_s, sgb_s = _sample_in(x_sample[:, 0], norm_mix, w_in_b, ln_g, ln_b, w00, b0, w_pa_b)
    caches_t = [c[0].transpose(0, 2, 3, 4, 1) for c in caches]
    bt_s = _sample_attn(qkvt_s, caches_t, bb=2)
    hist = state_conv_ffn[0]
    y_s, up_s = _sample_ffn(x_sample[:, 0], ma_s, sgb_s, bt_s, hist[:, 0], hist[:, 1],
                            w_pb_b, w_out_b, norm_ffn, w_up_b, conv_w[0], conv_b2, w_dn_b, nfin)

    kv_sample = [jnp.stack([kv_s[:, g * ATT_G:(g + 1) * ATT_G], kv_s[:, ATT_W + g * ATT_G: ATT_W + (g + 1) * ATT_G]],
                           axis=1).reshape(1, n_s, 1, 2, N_HG, HEAD_DIM) for g in range(N_DIL)]
    conv_sample = jnp.concatenate([hist[:, CONV_W - 2:], up_s[:, None]], axis=1)[None]

    return (y_prompt, y_s[:, None], kv_prompt[0], kv_sample[0], kv_prompt[1], kv_sample[1],
            kv_prompt[2], kv_sample[2], vn_last[None], vn_s[None, :, None], conv_prompt[None], conv_sample)
```

```python
import functools
import math

import jax
import jax.numpy as jnp
from jax import lax
from jax.experimental import pallas as pl
from jax.experimental.pallas import tpu as pltpu

F32 = jnp.float32
BF16 = jnp.bfloat16

HEAD_DIM = 64
N_HG = 4
ATT_G = N_HG * HEAD_DIM
DIL_GROUPS = ((128, 1), (512, 4), (2048, 16))
N_DIL = len(DIL_GROUPS)
ATT_W = N_DIL * ATT_G
BAND = 128
CHUNK = 128
N_GA = 4
GA_CH = 128
D_A = N_GA * GA_CH
CONV_W = 3
EPS = 1e-6
GELU_C = math.sqrt(2.0 / math.pi)

VMEM_LIMIT_BYTES = 56 * 1024 * 1024
FF_BLK = 768
LANES = 128


def _gelu(x):
    return x * (0.5 * (1.0 + jnp.tanh(GELU_C * (x + 0.044715 * (x * x * x)))))


def _rmsnorm(x, g):
    return x * lax.rsqrt(jnp.mean(x * x, axis=-1, keepdims=True) + EPS) * g


def _layernorm(x, g, b):
    mu = jnp.mean(x, axis=-1, keepdims=True)
    xc = x - mu
    var = jnp.mean(xc * xc, axis=-1, keepdims=True)
    return xc * lax.rsqrt(var + EPS) * g + b


def _dot(a, b):
    return jnp.dot(a, b, preferred_element_type=F32)


def _dot_nt(a, b):
    return lax.dot_general(a, b, (((1,), (1,)), ((), ())), preferred_element_type=F32)


def _const_spec(shape):
    nd = len(shape)
    return pl.BlockSpec(shape, lambda *_: (0,) * nd, pipeline_mode=pl.Buffered(1))


def _params(*sem):
    return pltpu.CompilerParams(dimension_semantics=sem, vmem_limit_bytes=VMEM_LIMIT_BYTES)


def _project(xn, w_ref, lo, hi):
    return _dot(xn, w_ref[:, lo:hi])


def _gates_and_branch_a(xn, w_ref, a_bf16, wpa_ref, d_model):
    base = 3 * ATT_W + 2 * D_A
    pa = _dot(a_bf16, wpa_ref[...])
    ga = _project(xn, w_ref, base, base + d_model)
    gb = _project(xn, w_ref, base + d_model, base + 2 * d_model)
    return jax.nn.sigmoid(ga) * pa, jax.nn.sigmoid(gb)


def _combine_groups(o_list):
    lses = [o[:, ATT_G:] for o in o_list]
    m = functools.reduce(jnp.maximum, lses)
    ws = [jnp.exp(l - m) for l in lses]
    den = functools.reduce(lambda a, b: a + b, ws)
    num = functools.reduce(lambda a, b: a + b, [w * o[:, :ATT_G] for w, o in zip(ws, o_list)])
    return num / den


def _conv_ffn(hn, history_fn, wup_ref, cw_ref, cb_ref, wdn_ref, d_ff):
    acc = None
    for j0 in range(0, d_ff, FF_BLK):
        wid = min(FF_BLK, d_ff - j0)
        halves = []
        for base in (0, d_ff):
            lo = base + j0
            u = _dot(hn, wup_ref[:, lo:lo + wid])
            prev1, prev2 = history_fn(u, lo)
            c = (cb_ref[:, lo:lo + wid] + cw_ref[2:3, lo:lo + wid] * u
                 + cw_ref[0:1, lo:lo + wid] * prev2
                 + cw_ref[1:2, lo:lo + wid] * prev1)
            halves.append(c)
        act = (_gelu(halves[0]) * halves[1]).astype(BF16)
        part = _dot(act, wdn_ref[j0:j0 + wid, :])
        acc = part if acc is None else acc + part
    return acc


def _mix_in_kernel(x_ref, g_ref, w_ref, lng_ref, lnb_ref, wsp_ref, bsp_ref, wpa_ref,
                   qkv0_ref, qkv1_ref, qkv2_ref, kvt_ref, ma_ref, sgb_ref, vn_ref, slab_ref, *, d_model):
    t = x_ref.shape[1]
    xn = _rmsnorm(x_ref[0], g_ref[...]).astype(BF16)
    q = _project(xn, w_ref, 0, ATT_W) * (HEAD_DIM ** -0.5)
    kv = _project(xn, w_ref, ATT_W, 3 * ATT_W)
    kvt_ref[0] = kv
    per_slab = ATT_G // LANES
    for g, ((_, dil), out_ref) in enumerate(zip(DIL_GROUPS, (qkv0_ref, qkv1_ref, qkv2_ref))):
        secs = [q[:, g * ATT_G:(g + 1) * ATT_G], kv[:, g * ATT_G:(g + 1) * ATT_G],
                kv[:, ATT_W + g * ATT_G:ATT_W + (g + 1) * ATT_G]]
        if dil == 1:
            out_ref[0, 0] = jnp.concatenate(secs, axis=1).astype(BF16)
            continue
        for i, sec in enumerate(secs):
            for k in range(per_slab):
                slab_ref[i * per_slab + k] = sec[:, k * LANES:(k + 1) * LANES]
        for r in range(dil):
            pieces = [slab_ref[n, pl.ds(r, t // dil, stride=dil), :] for n in range(3 * per_slab)]
            out_ref[0, r] = jnp.concatenate(pieces, axis=1).astype(BF16)

    u = _project(xn, w_ref, 3 * ATT_W, 3 * ATT_W + D_A)
    va = _project(xn, w_ref, 3 * ATT_W + D_A, 3 * ATT_W + 2 * D_A)
    vn = _layernorm(_gelu(va), lng_ref[...], lnb_ref[...])
    vn_ref[0] = vn[t - CHUNK:, :]
    vnb = vn.astype(BF16)
    row = lax.broadcasted_iota(jnp.int32, (CHUNK, CHUNK), 0)
    col = lax.broadcasted_iota(jnp.int32, (CHUNK, CHUNK), 1)
    w_m = [jnp.where(row >= col, wsp_ref[g], 0.0).astype(BF16) for g in range(N_GA)]
    rows = []
    for c in range(t // CHUNK):
        parts = [_dot(w_m[g], vnb[c * CHUNK:(c + 1) * CHUNK, g * GA_CH:(g + 1) * GA_CH]) + bsp_ref[g]
                 for g in range(N_GA)]
        rows.append(jnp.concatenate(parts, axis=1))
    mixed = jnp.concatenate(rows, axis=0)
    a = (_gelu(u) * mixed).astype(BF16)
    ma, sgb = _gates_and_branch_a(xn, w_ref, a, wpa_ref, d_model)
    ma_ref[0] = ma.astype(BF16)
    sgb_ref[0] = sgb.astype(BF16)


def _mix_in(x, norm_mix, w_in, ln_g, ln_b, w_sp, b_sp, w_pa, *, tile, tail):
    b, s, d = x.shape
    n_t = s // tile
    tail_blocks = tail // tile
    in_cols = w_in.shape[1]
    kern = functools.partial(_mix_in_kernel, d_model=d)
    return pl.pallas_call(
        kern,
        grid=(b, n_t),
        in_specs=[
            pl.BlockSpec((1, tile, d), lambda i, j: (i, j, 0)),
            _const_spec((1, d)),
            _const_spec((d, in_cols)),
            _const_spec((1, D_A)),
            _const_spec((1, D_A)),
            _const_spec((N_GA, CHUNK, CHUNK)),
            _const_spec((N_GA, CHUNK, 1)),
            _const_spec((D_A, d)),
        ],
        out_specs=[
            pl.BlockSpec((1, dil, tile // dil, 3 * ATT_G), lambda i, j: (i, 0, j, 0)) for _, dil in DIL_GROUPS
        ] + [
            pl.BlockSpec((1, tile, 2 * ATT_W), lambda i, j: (i, jnp.maximum(j - (n_t - tail_blocks), 0), 0)),
            pl.BlockSpec((1, tile, d), lambda i, j: (i, j, 0)),
            pl.BlockSpec((1, tile, d), lambda i, j: (i, j, 0)),
            pl.BlockSpec((1, CHUNK, D_A), lambda i, j: (i, 0, 0)),
        ],
        out_shape=[
            jax.ShapeDtypeStruct((b, dil, s // dil, 3 * ATT_G), BF16) for _, dil in DIL_GROUPS
        ] + [
            jax.ShapeDtypeStruct((b, tail, 2 * ATT_W), F32),
            jax.ShapeDtypeStruct((b, s, d), BF16),
            jax.ShapeDtypeStruct((b, s, d), BF16),
            jax.ShapeDtypeStruct((b, CHUNK, D_A), F32),
        ],
        scratch_shapes=[pltpu.VMEM((3 * ATT_G // LANES, tile, LANES), F32)],
        compiler_params=_params("arbitrary", "arbitrary"),
        name="prompt_mix_in",
    )(x, norm_mix, w_in, ln_g, ln_b, w_sp, b_sp, w_pa)


def _band_attn_kernel(cur_ref, halo_ref, o_ref):
    rows = cur_ref.shape[2]
    first = pl.program_id(2) == 0
    cur = cur_ref[0, 0]
    halo = halo_ref[0, 0]
    q_all = cur[:, 0:ATT_G]
    k_all = jnp.concatenate([halo[:, ATT_G:2 * ATT_G], cur[:, ATT_G:2 * ATT_G]], axis=0)
    v_all = jnp.concatenate([halo[:, 2 * ATT_G:], cur[:, 2 * ATT_G:]], axis=0)
    lane_head = lax.broadcasted_iota(jnp.int32, (1, ATT_G), 1) // HEAD_DIM
    zero = jnp.zeros((), BF16)
    k_h = [jnp.where(lane_head == h, k_all, zero) for h in range(N_HG)]
    v_h = [jnp.where(lane_head == h, v_all, zero) for h in range(N_HG)]
    qi = lax.broadcasted_iota(jnp.int32, (BAND, 2 * BAND), 0)
    ki = lax.broadcasted_iota(jnp.int32, (BAND, 2 * BAND), 1)
    band = jnp.logical_and(ki >= qi, ki <= qi + BAND)
    band_first = jnp.logical_and(band, jnp.logical_or(ki >= BAND, jnp.logical_not(first)))
    n_qb = rows // BAND
    tiles = []
    for qb in range(n_qb):
        q = q_all[qb * BAND:(qb + 1) * BAND]
        mask = band_first if qb == 0 else band
        for h in range(N_HG):
            s = _dot_nt(q, k_h[h][qb * BAND:(qb + 2) * BAND])
            tiles.append(jnp.where(mask, s, -jnp.inf))
    s_all = jnp.concatenate(tiles, axis=0)
    m = jnp.max(s_all, axis=-1, keepdims=True)
    p = jnp.exp(s_all - m)
    l = jnp.sum(p, axis=-1, keepdims=True)
    p_n = (p * (1.0 / l)).astype(BF16)
    lse = m + jnp.log(l)
    for qb in range(n_qb):
        o, lse_row = None, None
        for h in range(N_HG):
            u0 = (qb * N_HG + h) * BAND
            part = _dot(p_n[u0:u0 + BAND], v_h[h][qb * BAND:(qb + 2) * BAND])
            lse_part = jnp.where(lane_head == h, lse[u0:u0 + BAND], 0.0)
            o = part if o is None else o + part
            lse_row = lse_part if lse_row is None else lse_row + lse_part
        o_ref[0, 0, qb * BAND:(qb + 1) * BAND, :] = jnp.concatenate([o, lse_row], axis=1)


def _band_attn(qkv_g, *, rows):
    b, d, m, w = qkv_g.shape
    per = rows // BAND
    return pl.pallas_call(
        _band_attn_kernel,
        grid=(b, d, m // rows),
        in_specs=[
            pl.BlockSpec((1, 1, rows, w), lambda i, r, j: (i, r, j, 0)),
            pl.BlockSpec((1, 1, BAND, w), lambda i, r, j: (i, r, jnp.maximum(j * per - 1, 0), 0)),
        ],
        out_specs=pl.BlockSpec((1, 1, rows, 2 * ATT_G), lambda i, r, j: (i, r, j, 0)),
        out_shape=jax.ShapeDtypeStruct((b, d, m, 2 * ATT_G), F32),
        compiler_params=_params("arbitrary", "arbitrary", "arbitrary"),
        name=f"prompt_band_attn_d{d}",
    )(qkv_g, qkv_g)


def _merge_ffn_kernel(x_ref, ma_ref, sgb_ref, o0_ref, o1_ref, o2_ref, wpb_ref, wout_ref, nf_ref,
                      wup_ref, cw_ref, cb_ref, wdn_ref, nfin_ref, y_ref, ct_ref, carry_ref, hist_ref, tok_ref,
                      *, d_ff):
    t = x_ref.shape[1]

    @pl.when(pl.program_id(1) == 0)
    def _():
        carry_ref[...] = jnp.zeros_like(carry_ref)

    n_slab = 2 * ATT_G // LANES
    o_list = []
    for (_, dil), o_ref in zip(DIL_GROUPS, (o0_ref, o1_ref, o2_ref)):
        if dil == 1:
            o_list.append(o_ref[0, 0])
            continue
        for r in range(dil):
            for k in range(n_slab):
                tok_ref[k, pl.ds(r, t // dil, stride=dil), :] = o_ref[0, r, :, k * LANES:(k + 1) * LANES]
        o_list.append(jnp.concatenate([tok_ref[k] for k in range(n_slab)], axis=1))
    b_out = _combine_groups(o_list).astype(BF16)
    merged = ma_ref[0].astype(F32) + sgb_ref[0].astype(F32) * _dot(b_out, wpb_ref[...])
    h = x_ref[0] + _dot(merged.astype(BF16), wout_ref[...])
    hn = _rmsnorm(h, nf_ref[...]).astype(BF16)

    def history(u, lo):
        prev1, prev2 = [], []
        for k in range(u.shape[1] // LANES):
            cols = slice(lo + k * LANES, lo + (k + 1) * LANES)
            hist_ref[k, pl.ds(0, 8, stride=2), :] = carry_ref[:, cols]
            hist_ref[k, pl.ds(16, t, stride=2), :] = u[:, k * LANES:(k + 1) * LANES]
            carry_ref[:, cols] = u[t - 8:, k * LANES:(k + 1) * LANES]
            prev1.append(hist_ref[k, pl.ds(14, t, stride=2), :])
            prev2.append(hist_ref[k, pl.ds(12, t, stride=2), :])
        return jnp.concatenate(prev1, axis=1), jnp.concatenate(prev2, axis=1)

    ffn = _conv_ffn(hn, history, wup_ref, cw_ref, cb_ref, wdn_ref, d_ff)
    ct_ref[0] = carry_ref[8 - (CONV_W - 1):, :]
    y_ref[0] = _rmsnorm(h + ffn, nfin_ref[...])


def _merge_ffn(x, ma, sgb, o_list, w_pb, w_out, norm_ffn, w_up, conv_w, conv_b, w_dn, norm_final, *, tile):
    b, s, d = x.shape
    d_ff = w_dn.shape[0]
    tok = lambda w: pl.BlockSpec((1, tile, w), lambda i, j: (i, j, 0))
    kern = functools.partial(_merge_ffn_kernel, d_ff=d_ff)
    return pl.pallas_call(
        kern,
        grid=(b, s // tile),
        in_specs=[tok(d), tok(d), tok(d)] + [
                  pl.BlockSpec((1, dil, tile // dil, 2 * ATT_G), lambda i, j: (i, 0, j, 0)) for _, dil in DIL_GROUPS
                  ] + [
                  _const_spec((ATT_G, d)), _const_spec((d, d)), _const_spec((1, d)),
                  _const_spec((d, 2 * d_ff)), _const_spec((CONV_W, 2 * d_ff)), _const_spec((1, 2 * d_ff)),
                  _const_spec((d_ff, d)), _const_spec((1, d))],
        out_specs=[tok(d), pl.BlockSpec((1, CONV_W - 1, 2 * d_ff), lambda i, j: (i, 0, 0))],
        out_shape=[jax.ShapeDtypeStruct((b, s, d), F32),
                   jax.ShapeDtypeStruct((b, CONV_W - 1, 2 * d_ff), F32)],
        scratch_shapes=[pltpu.VMEM((8, 2 * d_ff), F32),
                        pltpu.VMEM((FF_BLK // LANES, 2 * (tile + 8), LANES), F32),
                        pltpu.VMEM((2 * ATT_G // LANES, tile, LANES), F32)],
        compiler_params=_params("arbitrary", "arbitrary"),
        name="prompt_merge_ffn",
    )(x, ma, sgb, *o_list, w_pb, w_out, norm_ffn, w_up, conv_w, conv_b, w_dn, norm_final)


def _sample_in_kernel(x_ref, g_ref, w_ref, lng_ref, lnb_ref, w00_ref, b0_ref, wpa_ref,
                      q_ref, kv_ref, vn_ref, ma_ref, sgb_ref, *, d_model):
    xn = _rmsnorm(x_ref[...], g_ref[...]).astype(BF16)
    q_ref[...] = _project(xn, w_ref, 0, ATT_W) * (HEAD_DIM ** -0.5)
    kv_ref[...] = _project(xn, w_ref, ATT_W, 3 * ATT_W)
    u = _project(xn, w_ref, 3 * ATT_W, 3 * ATT_W + D_A)
    va = _project(xn, w_ref, 3 * ATT_W + D_A, 3 * ATT_W + 2 * D_A)
    vn = _layernorm(_gelu(va), lng_ref[...], lnb_ref[...])
    vn_ref[...] = vn
    mixed = w00_ref[...] * vn + b0_ref[...]
    a = (_gelu(u) * mixed).astype(BF16)
    ma, sgb = _gates_and_branch_a(xn, w_ref, a, wpa_ref, d_model)
    ma_ref[...] = ma
    sgb_ref[...] = sgb


def _sample_in(x, norm_mix, w_in, ln_g, ln_b, w00, b0, w_pa):
    n, d = x.shape
    in_cols = w_in.shape[1]
    kern = functools.partial(_sample_in_kernel, d_model=d)
    full = lambda *shape: pl.BlockSpec(shape, lambda: (0,) * len(shape))
    return pl.pallas_call(
        kern,
        in_specs=[full(n, d), full(1, d), full(d, in_cols), full(1, D_A), full(1, D_A),
                  full(1, D_A), full(1, D_A), full(D_A, d)],
        out_specs=[full(n, ATT_W), full(n, 2 * ATT_W), full(n, D_A), full(n, d), full(n, d)],
        out_shape=[jax.ShapeDtypeStruct((n, ATT_W), F32), jax.ShapeDtypeStruct((n, 2 * ATT_W), F32),
                   jax.ShapeDtypeStruct((n, D_A), F32), jax.ShapeDtypeStruct((n, d), F32),
                   jax.ShapeDtypeStruct((n, d), F32)],
        compiler_params=pltpu.CompilerParams(vmem_limit_bytes=VMEM_LIMIT_BYTES),
        name="sample_in",
    )(x, norm_mix, w_in, ln_g, ln_b, w00, b0, w_pa)


def _sample_attn_kernel(q_ref, kv_ref, c0_ref, c1_ref, c2_ref, o_ref, *, bb):
    step = pl.program_id(0)
    c_refs = (c0_ref, c1_ref, c2_ref)
    sub = lax.broadcasted_iota(jnp.int32, (8, ATT_G), 0)
    lane_head = lax.broadcasted_iota(jnp.int32, (8, ATT_G), 1) // HEAD_DIM
    diag = sub == lane_head
    s_rows, s_new_rows, v_new_rows = [], [], []
    for bi in range(bb):
        b = step * bb + bi
        q_row = q_ref[pl.ds(b, 1), :]
        kv_row = kv_ref[pl.ds(b, 1), :]
        s_g, s_new_g, v_new_g = [], [], []
        for g, (c_ref, (_, dil)) in enumerate(zip(c_refs, DIL_GROUPS)):
            q_bd = jnp.where(diag, q_row[:, g * ATT_G:(g + 1) * ATT_G], 0.0)
            s = _dot(q_bd.astype(BF16), c_ref[bi, 0].astype(BF16))
            pos = lax.broadcasted_iota(jnp.int32, s.shape, 1)
            s_g.append(jnp.where(pos % dil == 0, s, -jnp.inf))
            s_new_g.append(jnp.sum(q_bd * kv_row[:, g * ATT_G:(g + 1) * ATT_G], axis=-1, keepdims=True))
            v_new_g.append(kv_row[:, ATT_W + g * ATT_G:ATT_W + (g + 1) * ATT_G])
        s_rows.append(jnp.concatenate(s_g, axis=1))
        s_new_rows.append(s_new_g)
        v_new_rows.append(v_new_g)
    s_all = jnp.concatenate(s_rows, axis=0)
    s_new = [jnp.concatenate([r[g] for r in s_new_rows], axis=0) for g in range(N_DIL)]
    m = functools.reduce(jnp.maximum, s_new + [jnp.max(s_all, axis=-1, keepdims=True)])
    p = jnp.exp(s_all - m)
    p_new = [jnp.exp(sn - m) for sn in s_new]
    den = functools.reduce(lambda a, b: a + b, p_new + [jnp.sum(p, axis=-1, keepdims=True)])
    for bi in range(bb):
        rows = slice(8 * bi, 8 * (bi + 1))
        num, lo = None, 0
        for g, c_ref in enumerate(c_refs):
            hi = lo + c_ref.shape[-1]
            part = (_dot_nt(p[rows, lo:hi].astype(BF16), c_ref[bi, 1].astype(BF16))
                    + p_new[g][rows] * v_new_rows[bi][g])
            num = part if num is None else num + part
            lo = hi
        o_row = jnp.sum(jnp.where(diag, num / den[rows], 0.0), axis=0, keepdims=True)
        o_ref[pl.ds(step * bb + bi, 1), :] = o_row


def _sample_attn(q, kv, caches_t, *, bb):
    n = q.shape[0]
    kern = functools.partial(_sample_attn_kernel, bb=bb)
    cache_specs = [pl.BlockSpec((bb,) + c.shape[1:], lambda i: (i, 0, 0, 0)) for c in caches_t]
    return pl.pallas_call(
        kern,
        grid=(n // bb,),
        in_specs=[pl.BlockSpec(q.shape, lambda i: (0, 0)), pl.BlockSpec(kv.shape, lambda i: (0, 0))] + cache_specs,
        out_specs=pl.BlockSpec((n, ATT_G), lambda i: (0, 0)),
        out_shape=jax.ShapeDtypeStruct((n, ATT_G), F32),
        compiler_params=_params("arbitrary"),
        name="sample_cache_attn",
    )(q, kv, *caches_t)


def _sample_ffn_kernel(x_ref, ma_ref, sgb_ref, b_ref, h0_ref, h1_ref, wpb_ref, wout_ref, nf_ref,
                       wup_ref, cw_ref, cb_ref, wdn_ref, nfin_ref, y_ref, up_ref, *, d_ff):
    merged = ma_ref[...] + sgb_ref[...] * _dot(b_ref[...].astype(BF16), wpb_ref[...])
    h = x_ref[...] + _dot(merged.astype(BF16), wout_ref[...])
    hn = _rmsnorm(h, nf_ref[...]).astype(BF16)

    def history(u, lo):
        up_ref[:, lo:lo + u.shape[1]] = u
        return h1_ref[:, lo:lo + u.shape[1]], h0_ref[:, lo:lo + u.shape[1]]

    ffn = _conv_ffn(hn, history, wup_ref, cw_ref, cb_ref, wdn_ref, d_ff)
    y_ref[...] = _rmsnorm(h + ffn, nfin_ref[...])


def _sample_ffn(x, ma, sgb, b_out, hist0, hist1, w_pb, w_out, norm_ffn, w_up, conv_w, conv_b, w_dn, norm_final):
    n, d = x.shape
    d_ff = w_dn.shape[0]
    kern = functools.partial(_sample_ffn_kernel, d_ff=d_ff)
    full = lambda *shape: pl.BlockSpec(shape, lambda: (0,) * len(shape))
    return pl.pallas_call(
        kern,
        in_specs=[full(n, d), full(n, d), full(n, d), full(n, ATT_G), full(n, 2 * d_ff), full(n, 2 * d_ff),
                  full(ATT_G, d), full(d, d), full(1, d), full(d, 2 * d_ff), full(CONV_W, 2 * d_ff),
                  full(1, 2 * d_ff), full(d_ff, d), full(1, d)],
        out_specs=[full(n, d), full(n, 2 * d_ff)],
        out_shape=[jax.ShapeDtypeStruct((n, d), F32), jax.ShapeDtypeStruct((n, 2 * d_ff), F32)],
        compiler_params=pltpu.CompilerParams(vmem_limit_bytes=VMEM_LIMIT_BYTES),
        name="sample_merge_ffn",
    )(x, ma, sgb, b_out, hist0, hist1, w_pb, w_out, norm_ffn, w_up, conv_w, conv_b, w_dn, norm_final)


def kernel(x_prompt, x_sample, cache_kv_w128, cache_kv_w512, cache_kv_w2048, state_conv_ffn, norm_mix, w_in, ln_v_gain, ln_v_bias, w_spatial, b_spatial, w_proj_a, w_proj_b, w_out, norm_ffn, w_up, conv_w, conv_b, w_down, norm_final):
    b, s, d = x_prompt.shape
    n_s, t_s, _ = x_sample.shape
    depth = w_in.shape[0]
    caches = (cache_kv_w128, cache_kv_w512, cache_kv_w2048)
    max_win = max(w for w, _ in DIL_GROUPS)
    assert depth == 1 and t_s == 1, "one layer, one new token per sample sequence"
    assert s % max_win == 0, "prompt length must cover whole strided query blocks of every dilation group"
    for (win, _), c in zip(DIL_GROUPS, caches):
        assert c.shape[2] == win, "cache must hold the whole window"

    w_in_b = w_in[0].astype(BF16)
    w_pa_b = w_proj_a[0].astype(BF16)
    w_pb_b = w_proj_b[0].astype(BF16)
    w_out_b = w_out[0].astype(BF16)
    w_up_b = w_up[0].astype(BF16)
    w_dn_b = w_down[0].astype(BF16)
    ln_g = ln_v_gain[0][None]
    ln_b = ln_v_bias[0][None]
    conv_b2 = conv_b[0][None]
    nfin = norm_final[None]

    *qkv_groups, kv_tail, ma, sgb, vn_last = _mix_in(
        x_prompt, norm_mix, w_in_b, ln_g, ln_b, w_spatial[0], b_spatial[0][:, :, None], w_pa_b,
        tile=512, tail=max_win)
    o_list = [_band_attn(qkv_g, rows=min(512, s // dil)) for qkv_g, (_, dil) in zip(qkv_groups, DIL_GROUPS)]
    y_prompt, conv_prompt = _merge_ffn(x_prompt, ma, sgb, o_list, w_pb_b, w_out_b, norm_ffn, w_up_b,
                                       conv_w[0], conv_b2, w_dn_b, nfin, tile=512)

    kv_prompt = []
    for g, (win, _) in enumerate(DIL_GROUPS):
        k_g = kv_tail[:, max_win - win:, g * ATT_G:(g + 1) * ATT_G]
        v_g = kv_tail[:, max_win - win:, ATT_W + g * ATT_G: ATT_W + (g + 1) * ATT_G]
        kv_prompt.append(jnp.stack([k_g, v_g], axis=2).reshape(1, b, win, 2, N_HG, HEAD_DIM))

    w00 = jnp.repeat(w_spatial[0][:, 0, 0], GA_CH)[None]
    b0 = jnp.repeat(b_spatial[0][:, 0], GA_CH)[None]
    q_s, kv_s, vn_s, ma_s, sgb_s = _sample_in(x_sample[:, 0], norm_mix, w_in_b, ln_g, ln_b, w00, b0, w_pa_b)
    caches_t = [c[0].transpose(0, 2, 3, 4, 1).reshape(n_s, 2, ATT_G, c.shape[2]) for c in caches]
    b_s = _sample_attn(q_s, kv_s, caches_t, bb=4)
    hist = state_conv_ffn[0]
    y_s, up_s = _sample_ffn(x_sample[:, 0], ma_s, sgb_s, b_s, hist[:, 0], hist[:, 1],
                            w_pb_b, w_out_b, norm_ffn, w_up_b, conv_w[0], conv_b2, w_dn_b, nfin)

    kv_sample = [jnp.stack([kv_s[:, g * ATT_G:(g + 1) * ATT_G], kv_s[:, ATT_W + g * ATT_G: ATT_W + (g + 1) * ATT_G]],
                           axis=1).reshape(1, n_s, 1, 2, N_HG, HEAD_DIM) for g in range(N_DIL)]
    conv_sample = jnp.concatenate([hist[:, CONV_W - 2:], up_s[:, None]], axis=1)[None]

    return (y_prompt, y_s[:, None], kv_prompt[0], kv_sample[0], kv_prompt[1], kv_sample[1],
            kv_prompt[2], kv_sample[2], vn_last[None], vn_s[None, :, None], conv_prompt[None], conv_sample)
```

```python
import functools
import math
from typing import NamedTuple

import jax
import jax.numpy as jnp
from jax import lax
from jax.experimental import pallas as pl
from jax.experimental.pallas import tpu as pltpu

F32 = jnp.float32
BF16 = jnp.bfloat16

HEAD_DIM = 64
N_HG = 4
ATT_G = N_HG * HEAD_DIM
DIL_GROUPS = ((128, 1), (512, 4), (2048, 16))
N_DIL = len(DIL_GROUPS)
ATT_W = N_DIL * ATT_G
BAND = 128
CHUNK = 128
N_GA = 4
GA_CH = 128
D_A = N_GA * GA_CH
CONV_W = 3
EPS = 1e-6
GELU_C = math.sqrt(2.0 / math.pi)

VMEM_LIMIT_BYTES = 56 * 1024 * 1024
FF_BLK = 768
LANES = 128


def _gelu(x):
    return x * (0.5 * (1.0 + jnp.tanh(GELU_C * (x + 0.044715 * (x * x * x)))))


def _rmsnorm(x, g):
    return x * lax.rsqrt(jnp.mean(x * x, axis=-1, keepdims=True) + EPS) * g


def _layernorm(x, g, b):
    mu = jnp.mean(x, axis=-1, keepdims=True)
    xc = x - mu
    var = jnp.mean(xc * xc, axis=-1, keepdims=True)
    return xc * lax.rsqrt(var + EPS) * g + b


def _dot(a, b):
    return jnp.dot(a, b, preferred_element_type=F32)


def _dot_nt(a, b):
    return lax.dot_general(a, b, (((1,), (1,)), ((), ())), preferred_element_type=F32)


def _const_spec(shape):
    nd = len(shape)
    return pl.BlockSpec(shape, lambda *_: (0,) * nd, pipeline_mode=pl.Buffered(1))


def _params(*sem):
    return pltpu.CompilerParams(dimension_semantics=sem, vmem_limit_bytes=VMEM_LIMIT_BYTES)


def _project(xn, w_ref, lo, hi):
    return _dot(xn, w_ref[:, lo:hi])


def _gates_and_branch_a(xn, w_ref, a_bf16, wpa_ref, d_model):
    base = 3 * ATT_W + 2 * D_A
    pa = _dot(a_bf16, wpa_ref[...])
    ga = _project(xn, w_ref, base, base + d_model)
    gb = _project(xn, w_ref, base + d_model, base + 2 * d_model)
    return jax.nn.sigmoid(ga) * pa, jax.nn.sigmoid(gb)


def _combine_groups(o_list):
    lses = [o[:, ATT_G:] for o in o_list]
    m = functools.reduce(jnp.maximum, lses)
    ws = [jnp.exp(l - m) for l in lses]
    den = functools.reduce(lambda a, b: a + b, ws)
    num = functools.reduce(lambda a, b: a + b, [w * o[:, :ATT_G] for w, o in zip(ws, o_list)])
    return num / den


def _conv_ffn(hn, history_fn, wup_ref, cw_ref, cb_ref, wdn_ref, d_ff):
    acc = None
    for j0 in range(0, d_ff, FF_BLK):
        wid = min(FF_BLK, d_ff - j0)
        halves = []
        for base in (0, d_ff):
            lo = base + j0
            u = _dot(hn, wup_ref[:, lo:lo + wid])
            prev1, prev2 = history_fn(u, lo)
            c = (cb_ref[:, lo:lo + wid] + cw_ref[2:3, lo:lo + wid] * u
                 + cw_ref[0:1, lo:lo + wid] * prev2
                 + cw_ref[1:2, lo:lo + wid] * prev1)
            halves.append(c)
        act = (_gelu(halves[0]) * halves[1]).astype(BF16)
        part = _dot(act, wdn_ref[j0:j0 + wid, :])
        acc = part if acc is None else acc + part
    return acc


def _cache_attend(q_row, kv_row, c_refs):
    sub = lax.broadcasted_iota(jnp.int32, (8, ATT_G), 0)
    lane_head = lax.broadcasted_iota(jnp.int32, (8, ATT_G), 1) // HEAD_DIM
    diag = sub == lane_head
    s_g, s_new, v_new = [], [], []
    for g, (c_ref, (_, dil)) in enumerate(zip(c_refs, DIL_GROUPS)):
        q_bd = jnp.where(diag, q_row[:, g * ATT_G:(g + 1) * ATT_G], 0.0)
        s = _dot(q_bd.astype(BF16), c_ref[0, 0].astype(BF16))
        pos = lax.broadcasted_iota(jnp.int32, s.shape, 1)
        s_g.append(jnp.where(pos % dil == 0, s, -jnp.inf))
        s_new.append(jnp.sum(q_bd * kv_row[:, g * ATT_G:(g + 1) * ATT_G], axis=-1, keepdims=True))
        v_new.append(kv_row[:, ATT_W + g * ATT_G:ATT_W + (g + 1) * ATT_G])
    s_all = jnp.concatenate(s_g, axis=1)
    m = functools.reduce(jnp.maximum, s_new + [jnp.max(s_all, axis=-1, keepdims=True)])
    p = jnp.exp(s_all - m)
    p_new = [jnp.exp(sn - m) for sn in s_new]
    den = functools.reduce(lambda a, b: a + b, p_new + [jnp.sum(p, axis=-1, keepdims=True)])
    num, lo = None, 0
    for g, c_ref in enumerate(c_refs):
        hi = lo + c_ref.shape[-1]
        part = _dot_nt(p[:, lo:hi].astype(BF16), c_ref[0, 1].astype(BF16)) + p_new[g] * v_new[g]
        num = part if num is None else num + part
        lo = hi
    return jnp.sum(jnp.where(diag, num / den, 0.0), axis=0, keepdims=True)


class _Rider(NamedTuple):
    offset: int
    count: int
    steps: int


def _rider_operands(rider, lin_fn, q_s, kv_s, caches_t):
    n = q_s.shape[0]

    def cache_map(*ids):
        return (jnp.minimum(rider.offset + lin_fn(*ids), n - 1), 0, 0, 0)

    in_specs = [pl.BlockSpec(q_s.shape, lambda *ids: (0, 0)), pl.BlockSpec(kv_s.shape, lambda *ids: (0, 0))]
    in_specs += [pl.BlockSpec((1,) + c.shape[1:], cache_map) for c in caches_t]
    out_spec = pl.BlockSpec((rider.count, ATT_G), lambda *ids: (0, 0))
    out_shape = jax.ShapeDtypeStruct((rider.count, ATT_G), F32)
    return in_specs, out_spec, out_shape, [q_s, kv_s, *caches_t]


def _ride(rider, lin, qs_ref, kvs_ref, c_refs, os_ref):
    def work():
        seq = rider.offset + lin
        os_ref[pl.ds(lin, 1), :] = _cache_attend(qs_ref[pl.ds(seq, 1), :], kvs_ref[pl.ds(seq, 1), :], c_refs)

    if rider.count == rider.steps:
        work()
    else:
        pl.when(lin < rider.count)(work)


def _mix_in_kernel(x_ref, g_ref, w_ref, lng_ref, lnb_ref, wsp_ref, bsp_ref, wpa_ref, *rest, d_model, rider):
    if rider is not None:
        qs_ref, kvs_ref, *c_refs = rest[:2 + N_DIL]
        rest = rest[2 + N_DIL:]
        qkv0_ref, qkv1_ref, qkv2_ref, kvt_ref, ma_ref, sgb_ref, vn_ref, os_ref, slab_ref = rest
        _ride(rider, pl.program_id(0) * pl.num_programs(1) + pl.program_id(1), qs_ref, kvs_ref, c_refs, os_ref)
    else:
        qkv0_ref, qkv1_ref, qkv2_ref, kvt_ref, ma_ref, sgb_ref, vn_ref, slab_ref = rest
    t = x_ref.shape[1]
    xn = _rmsnorm(x_ref[0], g_ref[...]).astype(BF16)
    q = _project(xn, w_ref, 0, ATT_W) * (HEAD_DIM ** -0.5)
    kv = _project(xn, w_ref, ATT_W, 3 * ATT_W)
    kvt_ref[0] = kv
    per_slab = ATT_G // LANES
    for g, ((_, dil), out_ref) in enumerate(zip(DIL_GROUPS, (qkv0_ref, qkv1_ref, qkv2_ref))):
        secs = [q[:, g * ATT_G:(g + 1) * ATT_G], kv[:, g * ATT_G:(g + 1) * ATT_G],
                kv[:, ATT_W + g * ATT_G:ATT_W + (g + 1) * ATT_G]]
        if dil == 1:
            out_ref[0, 0] = jnp.concatenate(secs, axis=1).astype(BF16)
            continue
        for i, sec in enumerate(secs):
            for k in range(per_slab):
                slab_ref[i * per_slab + k] = sec[:, k * LANES:(k + 1) * LANES]
        for r in range(dil):
            pieces = [slab_ref[n, pl.ds(r, t // dil, stride=dil), :] for n in range(3 * per_slab)]
            out_ref[0, r] = jnp.concatenate(pieces, axis=1).astype(BF16)

    u = _project(xn, w_ref, 3 * ATT_W, 3 * ATT_W + D_A)
    va = _project(xn, w_ref, 3 * ATT_W + D_A, 3 * ATT_W + 2 * D_A)
    vn = _layernorm(_gelu(va), lng_ref[...], lnb_ref[...])
    vn_ref[0] = vn[t - CHUNK:, :]
    vnb = vn.astype(BF16)
    row = lax.broadcasted_iota(jnp.int32, (CHUNK, CHUNK), 0)
    col = lax.broadcasted_iota(jnp.int32, (CHUNK, CHUNK), 1)
    w_m = [jnp.where(row >= col, wsp_ref[g], 0.0).astype(BF16) for g in range(N_GA)]
    rows = []
    for c in range(t // CHUNK):
        parts = [_dot(w_m[g], vnb[c * CHUNK:(c + 1) * CHUNK, g * GA_CH:(g + 1) * GA_CH]) + bsp_ref[g]
                 for g in range(N_GA)]
        rows.append(jnp.concatenate(parts, axis=1))
    mixed = jnp.concatenate(rows, axis=0)
    a = (_gelu(u) * mixed).astype(BF16)
    ma, sgb = _gates_and_branch_a(xn, w_ref, a, wpa_ref, d_model)
    ma_ref[0] = ma.astype(BF16)
    sgb_ref[0] = sgb.astype(BF16)


def _mix_in(x, norm_mix, w_in, ln_g, ln_b, w_sp, b_sp, w_pa, *, tile, tail, rider, sample):
    b, s, d = x.shape
    n_t = s // tile
    tail_blocks = tail // tile
    in_cols = w_in.shape[1]
    kern = functools.partial(_mix_in_kernel, d_model=d, rider=rider)
    in_specs = [
        pl.BlockSpec((1, tile, d), lambda i, j: (i, j, 0)),
        _const_spec((1, d)),
        _const_spec((d, in_cols)),
        _const_spec((1, D_A)),
        _const_spec((1, D_A)),
        _const_spec((N_GA, CHUNK, CHUNK)),
        _const_spec((N_GA, CHUNK, 1)),
        _const_spec((D_A, d)),
    ]
    out_specs = [
        pl.BlockSpec((1, dil, tile // dil, 3 * ATT_G), lambda i, j: (i, 0, j, 0)) for _, dil in DIL_GROUPS
    ] + [
        pl.BlockSpec((1, tile, 2 * ATT_W), lambda i, j: (i, jnp.maximum(j - (n_t - tail_blocks), 0), 0)),
        pl.BlockSpec((1, tile, d), lambda i, j: (i, j, 0)),
        pl.BlockSpec((1, tile, d), lambda i, j: (i, j, 0)),
        pl.BlockSpec((1, CHUNK, D_A), lambda i, j: (i, 0, 0)),
    ]
    out_shape = [
        jax.ShapeDtypeStruct((b, dil, s // dil, 3 * ATT_G), BF16) for _, dil in DIL_GROUPS
    ] + [
        jax.ShapeDtypeStruct((b, tail, 2 * ATT_W), F32),
        jax.ShapeDtypeStruct((b, s, d), BF16),
        jax.ShapeDtypeStruct((b, s, d), BF16),
        jax.ShapeDtypeStruct((b, CHUNK, D_A), F32),
    ]
    args = [x, norm_mix, w_in, ln_g, ln_b, w_sp, b_sp, w_pa]
    if rider is not None:
        r_in, r_out, r_shape, r_args = _rider_operands(rider, lambda i, j: i * n_t + j, *sample)
        in_specs, out_specs, out_shape, args = in_specs + r_in, out_specs + [r_out], out_shape + [r_shape], args + r_args
    return pl.pallas_call(
        kern,
        grid=(b, n_t),
        in_specs=in_specs,
        out_specs=out_specs,
        out_shape=out_shape,
        scratch_shapes=[pltpu.VMEM((3 * ATT_G // LANES, tile, LANES), F32)],
        compiler_params=_params("arbitrary", "arbitrary"),
        name="prompt_mix_in",
    )(*args)


def _band_attn_kernel(cur_ref, halo_ref, *rest, rider):
    if rider is not None:
        qs_ref, kvs_ref, *c_refs = rest[:2 + N_DIL]
        o_ref, os_ref = rest[2 + N_DIL:]
        lin = (pl.program_id(0) * pl.num_programs(1) + pl.program_id(1)) * pl.num_programs(2) + pl.program_id(2)
        _ride(rider, lin, qs_ref, kvs_ref, c_refs, os_ref)
    else:
        (o_ref,) = rest
    rows = cur_ref.shape[2]
    first = pl.program_id(2) == 0
    cur = cur_ref[0, 0]
    halo = halo_ref[0, 0]
    q_all = cur[:, 0:ATT_G]
    k_all = jnp.concatenate([halo[:, ATT_G:2 * ATT_G], cur[:, ATT_G:2 * ATT_G]], axis=0)
    v_all = jnp.concatenate([halo[:, 2 * ATT_G:], cur[:, 2 * ATT_G:]], axis=0)
    lane_head = lax.broadcasted_iota(jnp.int32, (1, ATT_G), 1) // HEAD_DIM
    zero = jnp.zeros((), BF16)
    k_h = [jnp.where(lane_head == h, k_all, zero) for h in range(N_HG)]
    v_h = [jnp.where(lane_head == h, v_all, zero) for h in range(N_HG)]
    qi = lax.broadcasted_iota(jnp.int32, (BAND, 2 * BAND), 0)
    ki = lax.broadcasted_iota(jnp.int32, (BAND, 2 * BAND), 1)
    band = jnp.logical_and(ki >= qi, ki <= qi + BAND)
    band_first = jnp.logical_and(band, jnp.logical_or(ki >= BAND, jnp.logical_not(first)))
    n_qb = rows // BAND
    tiles = []
    for qb in range(n_qb):
        q = q_all[qb * BAND:(qb + 1) * BAND]
        mask = band_first if qb == 0 else band
        for h in range(N_HG):
            s = _dot_nt(q, k_h[h][qb * BAND:(qb + 2) * BAND])
            tiles.append(jnp.where(mask, s, -jnp.inf))
    s_all = jnp.concatenate(tiles, axis=0)
    m = jnp.max(s_all, axis=-1, keepdims=True)
    p = jnp.exp(s_all - m)
    l = jnp.sum(p, axis=-1, keepdims=True)
    p_n = (p * (1.0 / l)).astype(BF16)
    lse = m + jnp.log(l)
    for qb in range(n_qb):
        o, lse_row = None, None
        for h in range(N_HG):
            u0 = (qb * N_HG + h) * BAND
            part = _dot(p_n[u0:u0 + BAND], v_h[h][qb * BAND:(qb + 2) * BAND])
            lse_part = jnp.where(lane_head == h, lse[u0:u0 + BAND], 0.0)
            o = part if o is None else o + part
            lse_row = lse_part if lse_row is None else lse_row + lse_part
        o_ref[0, 0, qb * BAND:(qb + 1) * BAND, :] = jnp.concatenate([o, lse_row], axis=1)


def _band_attn(qkv_g, *, rows, rider, sample):
    b, d, m, w = qkv_g.shape
    per = rows // BAND
    n_j = m // rows
    in_specs = [
        pl.BlockSpec((1, 1, rows, w), lambda i, r, j: (i, r, j, 0)),
        pl.BlockSpec((1, 1, BAND, w), lambda i, r, j: (i, r, jnp.maximum(j * per - 1, 0), 0)),
    ]
    out_specs = [pl.BlockSpec((1, 1, rows, 2 * ATT_G), lambda i, r, j: (i, r, j, 0))]
    out_shape = [jax.ShapeDtypeStruct((b, d, m, 2 * ATT_G), F32)]
    args = [qkv_g, qkv_g]
    if rider is not None:
        r_in, r_out, r_shape, r_args = _rider_operands(rider, lambda i, r, j: (i * d + r) * n_j + j, *sample)
        in_specs, out_specs, out_shape, args = in_specs + r_in, out_specs + [r_out], out_shape + [r_shape], args + r_args
    return pl.pallas_call(
        functools.partial(_band_attn_kernel, rider=rider),
        grid=(b, d, n_j),
        in_specs=in_specs,
        out_specs=out_specs,
        out_shape=out_shape,
        compiler_params=_params("arbitrary", "arbitrary", "arbitrary"),
        name=f"prompt_band_attn_d{d}",
    )(*args)


def _merge_ffn_kernel(x_ref, ma_ref, sgb_ref, o0_ref, o1_ref, o2_ref, wpb_ref, wout_ref, nf_ref,
                      wup_ref, cw_ref, cb_ref, wdn_ref, nfin_ref, y_ref, ct_ref, carry_ref, hist_ref, tok_ref,
                      *, d_ff):
    t = x_ref.shape[1]

    @pl.when(pl.program_id(1) == 0)
    def _():
        carry_ref[...] = jnp.zeros_like(carry_ref)

    n_slab = 2 * ATT_G // LANES
    o_list = []
    for (_, dil), o_ref in zip(DIL_GROUPS, (o0_ref, o1_ref, o2_ref)):
        if dil == 1:
            o_list.append(o_ref[0, 0])
            continue
        for r in range(dil):
            for k in range(n_slab):
                tok_ref[k, pl.ds(r, t // dil, stride=dil), :] = o_ref[0, r, :, k * LANES:(k + 1) * LANES]
        o_list.append(jnp.concatenate([tok_ref[k] for k in range(n_slab)], axis=1))
    b_out = _combine_groups(o_list).astype(BF16)
    merged = ma_ref[0].astype(F32) + sgb_ref[0].astype(F32) * _dot(b_out, wpb_ref[...])
    h = x_ref[0] + _dot(merged.astype(BF16), wout_ref[...])
    hn = _rmsnorm(h, nf_ref[...]).astype(BF16)

    def history(u, lo):
        prev1, prev2 = [], []
        for k in range(u.shape[1] // LANES):
            cols = slice(lo + k * LANES, lo + (k + 1) * LANES)
            hist_ref[k, pl.ds(0, 8, stride=2), :] = carry_ref[:, cols]
            hist_ref[k, pl.ds(16, t, stride=2), :] = u[:, k * LANES:(k + 1) * LANES]
            carry_ref[:, cols] = u[t - 8:, k * LANES:(k + 1) * LANES]
            prev1.append(hist_ref[k, pl.ds(14, t, stride=2), :])
            prev2.append(hist_ref[k, pl.ds(12, t, stride=2), :])
        return jnp.concatenate(prev1, axis=1), jnp.concatenate(prev2, axis=1)

    ffn = _conv_ffn(hn, history, wup_ref, cw_ref, cb_ref, wdn_ref, d_ff)
    ct_ref[0] = carry_ref[8 - (CONV_W - 1):, :]
    y_ref[0] = _rmsnorm(h + ffn, nfin_ref[...])


def _merge_ffn(x, ma, sgb, o_list, w_pb, w_out, norm_ffn, w_up, conv_w, conv_b, w_dn, norm_final, *, tile):
    b, s, d = x.shape
    d_ff = w_dn.shape[0]
    tok = lambda w: pl.BlockSpec((1, tile, w), lambda i, j: (i, j, 0))
    kern = functools.partial(_merge_ffn_kernel, d_ff=d_ff)
    return pl.pallas_call(
        kern,
        grid=(b, s // tile),
        in_specs=[tok(d), tok(d), tok(d)] + [
                  pl.BlockSpec((1, dil, tile // dil, 2 * ATT_G), lambda i, j: (i, 0, j, 0)) for _, dil in DIL_GROUPS
                  ] + [
                  _const_spec((ATT_G, d)), _const_spec((d, d)), _const_spec((1, d)),
                  _const_spec((d, 2 * d_ff)), _const_spec((CONV_W, 2 * d_ff)), _const_spec((1, 2 * d_ff)),
                  _const_spec((d_ff, d)), _const_spec((1, d))],
        out_specs=[tok(d), pl.BlockSpec((1, CONV_W - 1, 2 * d_ff), lambda i, j: (i, 0, 0))],
        out_shape=[jax.ShapeDtypeStruct((b, s, d), F32),
                   jax.ShapeDtypeStruct((b, CONV_W - 1, 2 * d_ff), F32)],
        scratch_shapes=[pltpu.VMEM((8, 2 * d_ff), F32),
                        pltpu.VMEM((FF_BLK // LANES, 2 * (tile + 8), LANES), F32),
                        pltpu.VMEM((2 * ATT_G // LANES, tile, LANES), F32)],
        compiler_params=_params("arbitrary", "arbitrary"),
        name="prompt_merge_ffn",
    )(x, ma, sgb, *o_list, w_pb, w_out, norm_ffn, w_up, conv_w, conv_b, w_dn, norm_final)


def _sample_in_kernel(x_ref, g_ref, w_ref, lng_ref, lnb_ref, w00_ref, b0_ref, wpa_ref,
                      q_ref, kv_ref, vn_ref, ma_ref, sgb_ref, *, d_model):
    xn = _rmsnorm(x_ref[...], g_ref[...]).astype(BF16)
    q_ref[...] = _project(xn, w_ref, 0, ATT_W) * (HEAD_DIM ** -0.5)
    kv_ref[...] = _project(xn, w_ref, ATT_W, 3 * ATT_W)
    u = _project(xn, w_ref, 3 * ATT_W, 3 * ATT_W + D_A)
    va = _project(xn, w_ref, 3 * ATT_W + D_A, 3 * ATT_W + 2 * D_A)
    vn = _layernorm(_gelu(va), lng_ref[...], lnb_ref[...])
    vn_ref[...] = vn
    mixed = w00_ref[...] * vn + b0_ref[...]
    a = (_gelu(u) * mixed).astype(BF16)
    ma, sgb = _gates_and_branch_a(xn, w_ref, a, wpa_ref, d_model)
    ma_ref[...] = ma
    sgb_ref[...] = sgb


def _sample_in(x, norm_mix, w_in, ln_g, ln_b, w00, b0, w_pa):
    n, d = x.shape
    in_cols = w_in.shape[1]
    kern = functools.partial(_sample_in_kernel, d_model=d)
    full = lambda *shape: pl.BlockSpec(shape, lambda: (0,) * len(shape))
    return pl.pallas_call(
        kern,
        in_specs=[full(n, d), full(1, d), full(d, in_cols), full(1, D_A), full(1, D_A),
                  full(1, D_A), full(1, D_A), full(D_A, d)],
        out_specs=[full(n, ATT_W), full(n, 2 * ATT_W), full(n, D_A), full(n, d), full(n, d)],
        out_shape=[jax.ShapeDtypeStruct((n, ATT_W), F32), jax.ShapeDtypeStruct((n, 2 * ATT_W), F32),
                   jax.ShapeDtypeStruct((n, D_A), F32), jax.ShapeDtypeStruct((n, d), F32),
                   jax.ShapeDtypeStruct((n, d), F32)],
        compiler_params=pltpu.CompilerParams(vmem_limit_bytes=VMEM_LIMIT_BYTES),
        name="sample_in",
    )(x, norm_mix, w_in, ln_g, ln_b, w00, b0, w_pa)


def _sample_attn_kernel(qs_ref, kvs_ref, *rest, rider):
    *c_refs, os_ref = rest
    _ride(rider, pl.program_id(0), qs_ref, kvs_ref, c_refs, os_ref)


def _sample_attn(rider, sample):
    r_in, r_out, r_shape, r_args = _rider_operands(rider, lambda i: i, *sample)
    return pl.pallas_call(
        functools.partial(_sample_attn_kernel, rider=rider),
        grid=(rider.steps,),
        in_specs=r_in,
        out_specs=r_out,
        out_shape=r_shape,
        compiler_params=_params("arbitrary"),
        name="sample_cache_attn",
    )(*r_args)


def _sample_ffn_kernel(x_ref, ma_ref, sgb_ref, b_ref, h0_ref, h1_ref, wpb_ref, wout_ref, nf_ref,
                       wup_ref, cw_ref, cb_ref, wdn_ref, nfin_ref, y_ref, up_ref, *, d_ff):
    merged = ma_ref[...] + sgb_ref[...] * _dot(b_ref[...].astype(BF16), wpb_ref[...])
    h = x_ref[...] + _dot(merged.astype(BF16), wout_ref[...])
    hn = _rmsnorm(h, nf_ref[...]).astype(BF16)

    def history(u, lo):
        up_ref[:, lo:lo + u.shape[1]] = u
        return h1_ref[:, lo:lo + u.shape[1]], h0_ref[:, lo:lo + u.shape[1]]

    ffn = _conv_ffn(hn, history, wup_ref, cw_ref, cb_ref, wdn_ref, d_ff)
    y_ref[...] = _rmsnorm(h + ffn, nfin_ref[...])


def _sample_ffn(x, ma, sgb, b_out, hist0, hist1, w_pb, w_out, norm_ffn, w_up, conv_w, conv_b, w_dn, norm_final):
    n, d = x.shape
    d_ff = w_dn.shape[0]
    kern = functools.partial(_sample_ffn_kernel, d_ff=d_ff)
    full = lambda *shape: pl.BlockSpec(shape, lambda: (0,) * len(shape))
    return pl.pallas_call(
        kern,
        in_specs=[full(n, d), full(n, d), full(n, d), full(n, ATT_G), full(n, 2 * d_ff), full(n, 2 * d_ff),
                  full(ATT_G, d), full(d, d), full(1, d), full(d, 2 * d_ff), full(CONV_W, 2 * d_ff),
                  full(1, 2 * d_ff), full(d_ff, d), full(1, d)],
        out_specs=[full(n, d), full(n, 2 * d_ff)],
        out_shape=[jax.ShapeDtypeStruct((n, d), F32), jax.ShapeDtypeStruct((n, 2 * d_ff), F32)],
        compiler_params=pltpu.CompilerParams(vmem_limit_bytes=VMEM_LIMIT_BYTES),
        name="sample_merge_ffn",
    )(x, ma, sgb, b_out, hist0, hist1, w_pb, w_out, norm_ffn, w_up, conv_w, conv_b, w_dn, norm_final)


def kernel(x_prompt, x_sample, cache_kv_w128, cache_kv_w512, cache_kv_w2048, state_conv_ffn, norm_mix, w_in, ln_v_gain, ln_v_bias, w_spatial, b_spatial, w_proj_a, w_proj_b, w_out, norm_ffn, w_up, conv_w, conv_b, w_down, norm_final):
    b, s, d = x_prompt.shape
    n_s, t_s, _ = x_sample.shape
    depth = w_in.shape[0]
    caches = (cache_kv_w128, cache_kv_w512, cache_kv_w2048)
    max_win = max(w for w, _ in DIL_GROUPS)
    assert depth == 1 and t_s == 1, "one layer, one new token per sample sequence"
    assert s % max_win == 0, "prompt length must cover whole strided query blocks of every dilation group"
    for (win, _), c in zip(DIL_GROUPS, caches):
        assert c.shape[2] == win, "cache must hold the whole window"

    w_in_b = w_in[0].astype(BF16)
    w_pa_b = w_proj_a[0].astype(BF16)
    w_pb_b = w_proj_b[0].astype(BF16)
    w_out_b = w_out[0].astype(BF16)
    w_up_b = w_up[0].astype(BF16)
    w_dn_b = w_down[0].astype(BF16)
    ln_g = ln_v_gain[0][None]
    ln_b = ln_v_bias[0][None]
    conv_b2 = conv_b[0][None]
    nfin = norm_final[None]

    w00 = jnp.repeat(w_spatial[0][:, 0, 0], GA_CH)[None]
    b0 = jnp.repeat(b_spatial[0][:, 0], GA_CH)[None]
    q_s, kv_s, vn_s, ma_s, sgb_s = _sample_in(x_sample[:, 0], norm_mix, w_in_b, ln_g, ln_b, w00, b0, w_pa_b)
    caches_t = [c[0].transpose(0, 2, 3, 4, 1).reshape(n_s, 2, ATT_G, c.shape[2]) for c in caches]
    sample = (q_s, kv_s, caches_t)

    mix_tile = 512
    band_rows = [min(512, s // dil) for _, dil in DIL_GROUPS]
    host_steps = [b * (s // mix_tile)] + [b * (s // rows) for rows in band_rows]
    riders, taken = [], 0
    for steps in host_steps:
        count = min(steps, n_s - taken)
        riders.append(_Rider(taken, count, steps) if count > 0 else None)
        taken += count

    mix_outs = _mix_in(
        x_prompt, norm_mix, w_in_b, ln_g, ln_b, w_spatial[0], b_spatial[0][:, :, None], w_pa_b,
        tile=mix_tile, tail=max_win, rider=riders[0], sample=sample)
    *qkv_groups, kv_tail, ma, sgb, vn_last = mix_outs[:N_DIL + 4]
    b_parts = list(mix_outs[N_DIL + 4:])
    o_list = []
    for qkv_g, rows, rider in zip(qkv_groups, band_rows, riders[1:]):
        outs = _band_attn(qkv_g, rows=rows, rider=rider, sample=sample)
        o_list.append(outs[0])
        b_parts.extend(outs[1:])
    if taken < n_s:
        b_parts.append(_sample_attn(_Rider(taken, n_s - taken, n_s - taken), sample))
    b_s = jnp.concatenate(b_parts, axis=0)
    y_prompt, conv_prompt = _merge_ffn(x_prompt, ma, sgb, o_list, w_pb_b, w_out_b, norm_ffn, w_up_b,
                                       conv_w[0], conv_b2, w_dn_b, nfin, tile=512)

    kv_prompt = []
    for g, (win, _) in enumerate(DIL_GROUPS):
        k_g = kv_tail[:, max_win - win:, g * ATT_G:(g + 1) * ATT_G]
        v_g = kv_tail[:, max_win - win:, ATT_W + g * ATT_G: ATT_W + (g + 1) * ATT_G]
        kv_prompt.append(jnp.stack([k_g, v_g], axis=2).reshape(1, b, win, 2, N_HG, HEAD_DIM))

    hist = state_conv_ffn[0]
    y_s, up_s = _sample_ffn(x_sample[:, 0], ma_s, sgb_s, b_s, hist[:, 0], hist[:, 1],
                            w_pb_b, w_out_b, norm_ffn, w_up_b, conv_w[0], conv_b2, w_dn_b, nfin)

    kv_sample = [jnp.stack([kv_s[:, g * ATT_G:(g + 1) * ATT_G], kv_s[:, ATT_W + g * ATT_G: ATT_W + (g + 1) * ATT_G]],
                           axis=1).reshape(1, n_s, 1, 2, N_HG, HEAD_DIM) for g in range(N_DIL)]
    conv_sample = jnp.concatenate([hist[:, CONV_W - 2:], up_s[:, None]], axis=1)[None]

    return (y_prompt, y_s[:, None], kv_prompt[0], kv_sample[0], kv_prompt[1], kv_sample[1],
            kv_prompt[2], kv_sample[2], vn_last[None], vn_s[None, :, None], conv_prompt[None], conv_sample)
```

```python
import functools
import math
from typing import NamedTuple

import jax
import jax.numpy as jnp
from jax import lax
from jax.experimental import pallas as pl
from jax.experimental.pallas import tpu as pltpu

F32 = jnp.float32
BF16 = jnp.bfloat16

HEAD_DIM = 64
N_HG = 4
ATT_G = N_HG * HEAD_DIM
DIL_GROUPS = ((128, 1), (512, 4), (2048, 16))
N_DIL = len(DIL_GROUPS)
ATT_W = N_DIL * ATT_G
BAND = 128
CHUNK = 128
N_GA = 4
GA_CH = 128
D_A = N_GA * GA_CH
CONV_W = 3
EPS = 1e-6
GELU_C = math.sqrt(2.0 / math.pi)

VMEM_LIMIT_BYTES = 62 * 1024 * 1024
FF_BLK = 768
LANES = 128
MIX_RIDERS_PER_STEP = 2


def _gelu(x):
    return x * (0.5 * (1.0 + jnp.tanh(GELU_C * (x + 0.044715 * (x * x * x)))))


def _rmsnorm(x, g):
    return x * lax.rsqrt(jnp.mean(x * x, axis=-1, keepdims=True) + EPS) * g


def _layernorm(x, g, b):
    mu = jnp.mean(x, axis=-1, keepdims=True)
    xc = x - mu
    var = jnp.mean(xc * xc, axis=-1, keepdims=True)
    return xc * lax.rsqrt(var + EPS) * g + b


def _dot(a, b):
    return jnp.dot(a, b, preferred_element_type=F32)


def _dot_nt(a, b):
    return lax.dot_general(a, b, (((1,), (1,)), ((), ())), preferred_element_type=F32)


def _const_spec(shape):
    nd = len(shape)
    return pl.BlockSpec(shape, lambda *_: (0,) * nd, pipeline_mode=pl.Buffered(1))


def _params(*sem):
    return pltpu.CompilerParams(dimension_semantics=sem, vmem_limit_bytes=VMEM_LIMIT_BYTES)


def _project(xn, w_ref, lo, hi):
    return _dot(xn, w_ref[:, lo:hi])


def _gates_and_branch_a(xn, w_ref, a_bf16, wpa_ref, d_model):
    base = 3 * ATT_W + 2 * D_A
    pa = _dot(a_bf16, wpa_ref[...])
    ga = _project(xn, w_ref, base, base + d_model)
    gb = _project(xn, w_ref, base + d_model, base + 2 * d_model)
    return jax.nn.sigmoid(ga) * pa, jax.nn.sigmoid(gb)


def _combine_groups(o_list):
    lses = [o[:, ATT_G:] for o in o_list]
    m = functools.reduce(jnp.maximum, lses)
    ws = [jnp.exp(l - m) for l in lses]
    den = functools.reduce(lambda a, b: a + b, ws)
    num = functools.reduce(lambda a, b: a + b, [w * o[:, :ATT_G] for w, o in zip(ws, o_list)])
    return num / den


def _conv_ffn(hn, history_fn, wup_ref, cw_ref, cb_ref, wdn_ref, d_ff):
    acc = None
    for j0 in range(0, d_ff, FF_BLK):
        wid = min(FF_BLK, d_ff - j0)
        halves = []
        for base in (0, d_ff):
            lo = base + j0
            u = _dot(hn, wup_ref[:, lo:lo + wid])
            prev1, prev2 = history_fn(u, lo)
            c = (cb_ref[:, lo:lo + wid] + cw_ref[2:3, lo:lo + wid] * u
                 + cw_ref[0:1, lo:lo + wid] * prev2
                 + cw_ref[1:2, lo:lo + wid] * prev1)
            halves.append(c)
        act = (_gelu(halves[0]) * halves[1]).astype(BF16)
        part = _dot(act, wdn_ref[j0:j0 + wid, :])
        acc = part if acc is None else acc + part
    return acc


def _cache_attend(q_row, kv_row, c_refs):
    sub = lax.broadcasted_iota(jnp.int32, (8, ATT_G), 0)
    lane_head = lax.broadcasted_iota(jnp.int32, (8, ATT_G), 1) // HEAD_DIM
    diag = sub == lane_head
    s_g, s_new, v_new = [], [], []
    for g, (c_ref, (_, dil)) in enumerate(zip(c_refs, DIL_GROUPS)):
        q_bd = jnp.where(diag, q_row[:, g * ATT_G:(g + 1) * ATT_G], 0.0)
        s = _dot(q_bd.astype(BF16), c_ref[0, 0].astype(BF16))
        pos = lax.broadcasted_iota(jnp.int32, s.shape, 1)
        s_g.append(jnp.where(pos % dil == 0, s, -jnp.inf))
        s_new.append(jnp.sum(q_bd * kv_row[:, g * ATT_G:(g + 1) * ATT_G], axis=-1, keepdims=True))
        v_new.append(kv_row[:, ATT_W + g * ATT_G:ATT_W + (g + 1) * ATT_G])
    s_all = jnp.concatenate(s_g, axis=1)
    m = functools.reduce(jnp.maximum, s_new + [jnp.max(s_all, axis=-1, keepdims=True)])
    p = jnp.exp(s_all - m)
    p_new = [jnp.exp(sn - m) for sn in s_new]
    den = functools.reduce(lambda a, b: a + b, p_new + [jnp.sum(p, axis=-1, keepdims=True)])
    num, lo = None, 0
    for g, c_ref in enumerate(c_refs):
        hi = lo + c_ref.shape[-1]
        part = _dot_nt(p[:, lo:hi].astype(BF16), c_ref[0, 1].astype(BF16)) + p_new[g] * v_new[g]
        num = part if num is None else num + part
        lo = hi
    return jnp.sum(jnp.where(diag, num / den, 0.0), axis=0, keepdims=True)


class _Rider(NamedTuple):
    offset: int
    count: int
    steps: int
    per_step: int

    @property
    def n_in(self):
        return 2 + N_DIL * self.per_step


def _rider_operands(rider, lin_fn, q_s, kv_s, caches_t):
    n = q_s.shape[0]

    def cache_map(k, *ids):
        return (jnp.minimum(rider.offset + lin_fn(*ids) * rider.per_step + k, n - 1), 0, 0, 0)

    in_specs = [pl.BlockSpec(q_s.shape, lambda *ids: (0, 0)), pl.BlockSpec(kv_s.shape, lambda *ids: (0, 0))]
    args = [q_s, kv_s]
    for k in range(rider.per_step):
        in_specs += [pl.BlockSpec((1,) + c.shape[1:], functools.partial(cache_map, k)) for c in caches_t]
        args += caches_t
    out_spec = pl.BlockSpec((rider.count, ATT_G), lambda *ids: (0, 0))
    out_shape = jax.ShapeDtypeStruct((rider.count, ATT_G), F32)
    return in_specs, out_spec, out_shape, args


def _ride(rider, lin, qs_ref, kvs_ref, c_refs, os_ref):
    for k in range(rider.per_step):
        local = lin * rider.per_step + k

        def work(local=local, k=k):
            seq = rider.offset + local
            os_ref[pl.ds(local, 1), :] = _cache_attend(qs_ref[pl.ds(seq, 1), :], kvs_ref[pl.ds(seq, 1), :],
                                                       c_refs[k * N_DIL:(k + 1) * N_DIL])

        if rider.count == rider.steps * rider.per_step:
            work()
        else:
            pl.when(local < rider.count)(work)


def _mix_in_kernel(x_ref, g_ref, w_ref, lng_ref, lnb_ref, wsp_ref, bsp_ref, wpa_ref, *rest, d_model, rider):
    if rider is not None:
        qs_ref, kvs_ref, *c_refs = rest[:rider.n_in]
        rest = rest[rider.n_in:]
        qkv0_ref, qkv1_ref, qkv2_ref, kvt_ref, ma_ref, sgb_ref, vn_ref, os_ref, slab_ref = rest
        _ride(rider, pl.program_id(0) * pl.num_programs(1) + pl.program_id(1), qs_ref, kvs_ref, c_refs, os_ref)
    else:
        qkv0_ref, qkv1_ref, qkv2_ref, kvt_ref, ma_ref, sgb_ref, vn_ref, slab_ref = rest
    t = x_ref.shape[1]
    xn = _rmsnorm(x_ref[0], g_ref[...]).astype(BF16)
    q = _project(xn, w_ref, 0, ATT_W) * (HEAD_DIM ** -0.5)
    kv = _project(xn, w_ref, ATT_W, 3 * ATT_W)
    kvt_ref[0] = kv
    per_slab = ATT_G // LANES
    for g, ((_, dil), out_ref) in enumerate(zip(DIL_GROUPS, (qkv0_ref, qkv1_ref, qkv2_ref))):
        secs = [q[:, g * ATT_G:(g + 1) * ATT_G], kv[:, g * ATT_G:(g + 1) * ATT_G],
                kv[:, ATT_W + g * ATT_G:ATT_W + (g + 1) * ATT_G]]
        if dil == 1:
            out_ref[0, 0] = jnp.concatenate(secs, axis=1).astype(BF16)
            continue
        for i, sec in enumerate(secs):
            for k in range(per_slab):
                slab_ref[i * per_slab + k] = sec[:, k * LANES:(k + 1) * LANES]
        for r in range(dil):
            pieces = [slab_ref[n, pl.ds(r, t // dil, stride=dil), :] for n in range(3 * per_slab)]
            out_ref[0, r] = jnp.concatenate(pieces, axis=1).astype(BF16)

    u = _project(xn, w_ref, 3 * ATT_W, 3 * ATT_W + D_A)
    va = _project(xn, w_ref, 3 * ATT_W + D_A, 3 * ATT_W + 2 * D_A)
    vn = _layernorm(_gelu(va), lng_ref[...], lnb_ref[...])
    vn_ref[0] = vn[t - CHUNK:, :]
    vnb = vn.astype(BF16)
    row = lax.broadcasted_iota(jnp.int32, (CHUNK, CHUNK), 0)
    col = lax.broadcasted_iota(jnp.int32, (CHUNK, CHUNK), 1)
    w_m = [jnp.where(row >= col, wsp_ref[g], 0.0).astype(BF16) for g in range(N_GA)]
    rows = []
    for c in range(t // CHUNK):
        parts = [_dot(w_m[g], vnb[c * CHUNK:(c + 1) * CHUNK, g * GA_CH:(g + 1) * GA_CH]) + bsp_ref[g]
                 for g in range(N_GA)]
        rows.append(jnp.concatenate(parts, axis=1))
    mixed = jnp.concatenate(rows, axis=0)
    a = (_gelu(u) * mixed).astype(BF16)
    ma, sgb = _gates_and_branch_a(xn, w_ref, a, wpa_ref, d_model)
    ma_ref[0] = ma.astype(BF16)
    sgb_ref[0] = sgb.astype(BF16)


def _mix_in(x, norm_mix, w_in, ln_g, ln_b, w_sp, b_sp, w_pa, *, tile, tail, rider, sample):
    b, s, d = x.shape
    n_t = s // tile
    tail_blocks = tail // tile
    in_cols = w_in.shape[1]
    kern = functools.partial(_mix_in_kernel, d_model=d, rider=rider)
    in_specs = [
        pl.BlockSpec((1, tile, d), lambda i, j: (i, j, 0)),
        _const_spec((1, d)),
        _const_spec((d, in_cols)),
        _const_spec((1, D_A)),
        _const_spec((1, D_A)),
        _const_spec((N_GA, CHUNK, CHUNK)),
        _const_spec((N_GA, CHUNK, 1)),
        _const_spec((D_A, d)),
    ]
    out_specs = [
        pl.BlockSpec((1, dil, tile // dil, 3 * ATT_G), lambda i, j: (i, 0, j, 0)) for _, dil in DIL_GROUPS
    ] + [
        pl.BlockSpec((1, tile, 2 * ATT_W), lambda i, j: (i, jnp.maximum(j - (n_t - tail_blocks), 0), 0)),
        pl.BlockSpec((1, tile, d), lambda i, j: (i, j, 0)),
        pl.BlockSpec((1, tile, d), lambda i, j: (i, j, 0)),
        pl.BlockSpec((1, CHUNK, D_A), lambda i, j: (i, 0, 0)),
    ]
    out_shape = [
        jax.ShapeDtypeStruct((b, dil, s // dil, 3 * ATT_G), BF16) for _, dil in DIL_GROUPS
    ] + [
        jax.ShapeDtypeStruct((b, tail, 2 * ATT_W), F32),
        jax.ShapeDtypeStruct((b, s, d), BF16),
        jax.ShapeDtypeStruct((b, s, d), BF16),
        jax.ShapeDtypeStruct((b, CHUNK, D_A), F32),
    ]
    args = [x, norm_mix, w_in, ln_g, ln_b, w_sp, b_sp, w_pa]
    if rider is not None:
        r_in, r_out, r_shape, r_args = _rider_operands(rider, lambda i, j: i * n_t + j, *sample)
        in_specs, out_specs, out_shape, args = in_specs + r_in, out_specs + [r_out], out_shape + [r_shape], args + r_args
    return pl.pallas_call(
        kern,
        grid=(b, n_t),
        in_specs=in_specs,
        out_specs=out_specs,
        out_shape=out_shape,
        scratch_shapes=[pltpu.VMEM((3 * ATT_G // LANES, tile, LANES), F32)],
        compiler_params=_params("arbitrary", "arbitrary"),
        name="prompt_mix_in",
    )(*args)


def _band_attn_kernel(cur_ref, halo_ref, *rest, rider):
    if rider is not None:
        qs_ref, kvs_ref, *c_refs = rest[:rider.n_in]
        o_ref, os_ref = rest[rider.n_in:]
        lin = (pl.program_id(0) * pl.num_programs(1) + pl.program_id(1)) * pl.num_programs(2) + pl.program_id(2)
        _ride(rider, lin, qs_ref, kvs_ref, c_refs, os_ref)
    else:
        (o_ref,) = rest
    rows = cur_ref.shape[2]
    first = pl.program_id(2) == 0
    cur = cur_ref[0, 0]
    halo = halo_ref[0, 0]
    q_all = cur[:, 0:ATT_G]
    k_all = jnp.concatenate([halo[:, ATT_G:2 * ATT_G], cur[:, ATT_G:2 * ATT_G]], axis=0)
    v_all = jnp.concatenate([halo[:, 2 * ATT_G:], cur[:, 2 * ATT_G:]], axis=0)
    lane_head = lax.broadcasted_iota(jnp.int32, (1, ATT_G), 1) // HEAD_DIM
    zero = jnp.zeros((), BF16)
    k_h = [jnp.where(lane_head == h, k_all, zero) for h in range(N_HG)]
    v_h = [jnp.where(lane_head == h, v_all, zero) for h in range(N_HG)]
    qi = lax.broadcasted_iota(jnp.int32, (BAND, 2 * BAND), 0)
    ki = lax.broadcasted_iota(jnp.int32, (BAND, 2 * BAND), 1)
    band = jnp.logical_and(ki >= qi, ki <= qi + BAND)
    band_first = jnp.logical_and(band, jnp.logical_or(ki >= BAND, jnp.logical_not(first)))
    n_qb = rows // BAND
    tiles = []
    for qb in range(n_qb):
        q = q_all[qb * BAND:(qb + 1) * BAND]
        mask = band_first if qb == 0 else band
        for h in range(N_HG):
            s = _dot_nt(q, k_h[h][qb * BAND:(qb + 2) * BAND])
            tiles.append(jnp.where(mask, s, -jnp.inf))
    s_all = jnp.concatenate(tiles, axis=0)
    m = jnp.max(s_all, axis=-1, keepdims=True)
    p = jnp.exp(s_all - m)
    l = jnp.sum(p, axis=-1, keepdims=True)
    p_n = (p * (1.0 / l)).astype(BF16)
    lse = m + jnp.log(l)
    for qb in range(n_qb):
        o, lse_row = None, None
        for h in range(N_HG):
            u0 = (qb * N_HG + h) * BAND
            part = _dot(p_n[u0:u0 + BAND], v_h[h][qb * BAND:(qb + 2) * BAND])
            lse_part = jnp.where(lane_head == h, lse[u0:u0 + BAND], 0.0)
            o = part if o is None else o + part
            lse_row = lse_part if lse_row is None else lse_row + lse_part
        o_ref[0, 0, qb * BAND:(qb + 1) * BAND, :] = jnp.concatenate([o, lse_row], axis=1)


def _band_attn(qkv_g, *, rows, rider, sample):
    b, d, m, w = qkv_g.shape
    per = rows // BAND
    n_j = m // rows
    in_specs = [
        pl.BlockSpec((1, 1, rows, w), lambda i, r, j: (i, r, j, 0)),
        pl.BlockSpec((1, 1, BAND, w), lambda i, r, j: (i, r, jnp.maximum(j * per - 1, 0), 0)),
    ]
    out_specs = [pl.BlockSpec((1, 1, rows, 2 * ATT_G), lambda i, r, j: (i, r, j, 0))]
    out_shape = [jax.ShapeDtypeStruct((b, d, m, 2 * ATT_G), F32)]
    args = [qkv_g, qkv_g]
    if rider is not None:
        r_in, r_out, r_shape, r_args = _rider_operands(rider, lambda i, r, j: (i * d + r) * n_j + j, *sample)
        in_specs, out_specs, out_shape, args = in_specs + r_in, out_specs + [r_out], out_shape + [r_shape], args + r_args
    return pl.pallas_call(
        functools.partial(_band_attn_kernel, rider=rider),
        grid=(b, d, n_j),
        in_specs=in_specs,
        out_specs=out_specs,
        out_shape=out_shape,
        compiler_params=_params("arbitrary", "arbitrary", "arbitrary"),
        name=f"prompt_band_attn_d{d}",
    )(*args)


def _merge_ffn_kernel(x_ref, ma_ref, sgb_ref, o0_ref, o1_ref, o2_ref, wpb_ref, wout_ref, nf_ref,
                      wup_ref, cw_ref, cb_ref, wdn_ref, nfin_ref, *rest, d_ff, rider):
    if rider is not None:
        qs_ref, kvs_ref, *c_refs = rest[:rider.n_in]
        y_ref, ct_ref, os_ref, carry_ref, hist_ref, tok_ref = rest[rider.n_in:]
    else:
        y_ref, ct_ref, carry_ref, hist_ref, tok_ref = rest
    t = x_ref.shape[1]

    @pl.when(pl.program_id(1) == 0)
    def _():
        carry_ref[...] = jnp.zeros_like(carry_ref)

    if rider is not None:
        _ride(rider, pl.program_id(0) * pl.num_programs(1) + pl.program_id(1), qs_ref, kvs_ref, c_refs, os_ref)

    n_slab = 2 * ATT_G // LANES
    o_list = []
    for (_, dil), o_ref in zip(DIL_GROUPS, (o0_ref, o1_ref, o2_ref)):
        if dil == 1:
            o_list.append(o_ref[0, 0])
            continue
        for r in range(dil):
            for k in range(n_slab):
                tok_ref[k, pl.ds(r, t // dil, stride=dil), :] = o_ref[0, r, :, k * LANES:(k + 1) * LANES]
        o_list.append(jnp.concatenate([tok_ref[k] for k in range(n_slab)], axis=1))
    b_out = _combine_groups(o_list).astype(BF16)
    merged = ma_ref[0].astype(F32) + sgb_ref[0].astype(F32) * _dot(b_out, wpb_ref[...])
    h = x_ref[0] + _dot(merged.astype(BF16), wout_ref[...])
    hn = _rmsnorm(h, nf_ref[...]).astype(BF16)

    def history(u, lo):
        prev1, prev2 = [], []
        for k in range(u.shape[1] // LANES):
            cols = slice(lo + k * LANES, lo + (k + 1) * LANES)
            hist_ref[k, pl.ds(0, 8, stride=2), :] = carry_ref[:, cols]
            hist_ref[k, pl.ds(16, t, stride=2), :] = u[:, k * LANES:(k + 1) * LANES]
            carry_ref[:, cols] = u[t - 8:, k * LANES:(k + 1) * LANES]
            prev1.append(hist_ref[k, pl.ds(14, t, stride=2), :])
            prev2.append(hist_ref[k, pl.ds(12, t, stride=2), :])
        return jnp.concatenate(prev1, axis=1), jnp.concatenate(prev2, axis=1)

    ffn = _conv_ffn(hn, history, wup_ref, cw_ref, cb_ref, wdn_ref, d_ff)
    ct_ref[0] = carry_ref[8 - (CONV_W - 1):, :]
    y_ref[0] = _rmsnorm(h + ffn, nfin_ref[...])


def _merge_ffn(x, ma, sgb, o_list, w_pb, w_out, norm_ffn, w_up, conv_w, conv_b, w_dn, norm_final, *, tile,
               rider, sample):
    b, s, d = x.shape
    d_ff = w_dn.shape[0]
    n_t = s // tile
    tok = lambda w: pl.BlockSpec((1, tile, w), lambda i, j: (i, j, 0))
    kern = functools.partial(_merge_ffn_kernel, d_ff=d_ff, rider=rider)
    in_specs = [tok(d), tok(d), tok(d)] + [
        pl.BlockSpec((1, dil, tile // dil, 2 * ATT_G), lambda i, j: (i, 0, j, 0)) for _, dil in DIL_GROUPS
    ] + [
        _const_spec((ATT_G, d)), _const_spec((d, d)), _const_spec((1, d)),
        _const_spec((d, 2 * d_ff)), _const_spec((CONV_W, 2 * d_ff)), _const_spec((1, 2 * d_ff)),
        _const_spec((d_ff, d)), _const_spec((1, d))]
    out_specs = [tok(d), pl.BlockSpec((1, CONV_W - 1, 2 * d_ff), lambda i, j: (i, 0, 0))]
    out_shape = [jax.ShapeDtypeStruct((b, s, d), F32), jax.ShapeDtypeStruct((b, CONV_W - 1, 2 * d_ff), F32)]
    args = [x, ma, sgb, *o_list, w_pb, w_out, norm_ffn, w_up, conv_w, conv_b, w_dn, norm_final]
    if rider is not None:
        r_in, r_out, r_shape, r_args = _rider_operands(rider, lambda i, j: i * n_t + j, *sample)
        in_specs, out_specs, out_shape, args = in_specs + r_in, out_specs + [r_out], out_shape + [r_shape], args + r_args
    return pl.pallas_call(
        kern,
        grid=(b, n_t),
        in_specs=in_specs,
        out_specs=out_specs,
        out_shape=out_shape,
        scratch_shapes=[pltpu.VMEM((8, 2 * d_ff), F32),
                        pltpu.VMEM((FF_BLK // LANES, 2 * (tile + 8), LANES), F32),
                        pltpu.VMEM((2 * ATT_G // LANES, tile, LANES), F32)],
        compiler_params=_params("arbitrary", "arbitrary"),
        name="prompt_merge_ffn",
    )(*args)


def _sample_in_kernel(x_ref, g_ref, w_ref, lng_ref, lnb_ref, w00_ref, b0_ref, wpa_ref,
                      q_ref, kv_ref, vn_ref, ma_ref, sgb_ref, *, d_model):
    xn = _rmsnorm(x_ref[...], g_ref[...]).astype(BF16)
    q_ref[...] = _project(xn, w_ref, 0, ATT_W) * (HEAD_DIM ** -0.5)
    kv_ref[...] = _project(xn, w_ref, ATT_W, 3 * ATT_W)
    u = _project(xn, w_ref, 3 * ATT_W, 3 * ATT_W + D_A)
    va = _project(xn, w_ref, 3 * ATT_W + D_A, 3 * ATT_W + 2 * D_A)
    vn = _layernorm(_gelu(va), lng_ref[...], lnb_ref[...])
    vn_ref[...] = vn
    mixed = w00_ref[...] * vn + b0_ref[...]
    a = (_gelu(u) * mixed).astype(BF16)
    ma, sgb = _gates_and_branch_a(xn, w_ref, a, wpa_ref, d_model)
    ma_ref[...] = ma
    sgb_ref[...] = sgb


def _sample_in(x, norm_mix, w_in, ln_g, ln_b, w00, b0, w_pa):
    n, d = x.shape
    in_cols = w_in.shape[1]
    kern = functools.partial(_sample_in_kernel, d_model=d)
    full = lambda *shape: pl.BlockSpec(shape, lambda: (0,) * len(shape))
    return pl.pallas_call(
        kern,
        in_specs=[full(n, d), full(1, d), full(d, in_cols), full(1, D_A), full(1, D_A),
                  full(1, D_A), full(1, D_A), full(D_A, d)],
        out_specs=[full(n, ATT_W), full(n, 2 * ATT_W), full(n, D_A), full(n, d), full(n, d)],
        out_shape=[jax.ShapeDtypeStruct((n, ATT_W), F32), jax.ShapeDtypeStruct((n, 2 * ATT_W), F32),
                   jax.ShapeDtypeStruct((n, D_A), F32), jax.ShapeDtypeStruct((n, d), F32),
                   jax.ShapeDtypeStruct((n, d), F32)],
        compiler_params=pltpu.CompilerParams(vmem_limit_bytes=VMEM_LIMIT_BYTES),
        name="sample_in",
    )(x, norm_mix, w_in, ln_g, ln_b, w00, b0, w_pa)


def _sample_attn_kernel(qs_ref, kvs_ref, *rest, rider):
    *c_refs, os_ref = rest
    _ride(rider, pl.program_id(0), qs_ref, kvs_ref, c_refs, os_ref)


def _sample_attn(rider, sample):
    r_in, r_out, r_shape, r_args = _rider_operands(rider, lambda i: i, *sample)
    return pl.pallas_call(
        functools.partial(_sample_attn_kernel, rider=rider),
        grid=(rider.steps,),
        in_specs=r_in,
        out_specs=r_out,
        out_shape=r_shape,
        compiler_params=_params("arbitrary"),
        name="sample_cache_attn",
    )(*r_args)


def _sample_ffn_kernel(x_ref, ma_ref, sgb_ref, b_ref, h0_ref, h1_ref, wpb_ref, wout_ref, nf_ref,
                       wup_ref, cw_ref, cb_ref, wdn_ref, nfin_ref, y_ref, up_ref, *, d_ff):
    merged = ma_ref[...] + sgb_ref[...] * _dot(b_ref[...].astype(BF16), wpb_ref[...])
    h = x_ref[...] + _dot(merged.astype(BF16), wout_ref[...])
    hn = _rmsnorm(h, nf_ref[...]).astype(BF16)

    def history(u, lo):
        up_ref[:, lo:lo + u.shape[1]] = u
        return h1_ref[:, lo:lo + u.shape[1]], h0_ref[:, lo:lo + u.shape[1]]

    ffn = _conv_ffn(hn, history, wup_ref, cw_ref, cb_ref, wdn_ref, d_ff)
    y_ref[...] = _rmsnorm(h + ffn, nfin_ref[...])


def _sample_ffn(x, ma, sgb, b_out, hist0, hist1, w_pb, w_out, norm_ffn, w_up, conv_w, conv_b, w_dn, norm_final):
    n, d = x.shape
    d_ff = w_dn.shape[0]
    kern = functools.partial(_sample_ffn_kernel, d_ff=d_ff)
    full = lambda *shape: pl.BlockSpec(shape, lambda: (0,) * len(shape))
    return pl.pallas_call(
        kern,
        in_specs=[full(n, d), full(n, d), full(n, d), full(n, ATT_G), full(n, 2 * d_ff), full(n, 2 * d_ff),
                  full(ATT_G, d), full(d, d), full(1, d), full(d, 2 * d_ff), full(CONV_W, 2 * d_ff),
                  full(1, 2 * d_ff), full(d_ff, d), full(1, d)],
        out_specs=[full(n, d), full(n, 2 * d_ff)],
        out_shape=[jax.ShapeDtypeStruct((n, d), F32), jax.ShapeDtypeStruct((n, 2 * d_ff), F32)],
        compiler_params=pltpu.CompilerParams(vmem_limit_bytes=VMEM_LIMIT_BYTES),
        name="sample_merge_ffn",
    )(x, ma, sgb, b_out, hist0, hist1, w_pb, w_out, norm_ffn, w_up, conv_w, conv_b, w_dn, norm_final)


def kernel(x_prompt, x_sample, cache_kv_w128, cache_kv_w512, cache_kv_w2048, state_conv_ffn, norm_mix, w_in, ln_v_gain, ln_v_bias, w_spatial, b_spatial, w_proj_a, w_proj_b, w_out, norm_ffn, w_up, conv_w, conv_b, w_down, norm_final):
    b, s, d = x_prompt.shape
    n_s, t_s, _ = x_sample.shape
    depth = w_in.shape[0]
    caches = (cache_kv_w128, cache_kv_w512, cache_kv_w2048)
    max_win = max(w for w, _ in DIL_GROUPS)
    assert depth == 1 and t_s == 1, "one layer, one new token per sample sequence"
    assert s % max_win == 0, "prompt length must cover whole strided query blocks of every dilation group"
    for (win, _), c in zip(DIL_GROUPS, caches):
        assert c.shape[2] == win, "cache must hold the whole window"

    w_in_b = w_in[0].astype(BF16)
    w_pa_b = w_proj_a[0].astype(BF16)
    w_pb_b = w_proj_b[0].astype(BF16)
    w_out_b = w_out[0].astype(BF16)
    w_up_b = w_up[0].astype(BF16)
    w_dn_b = w_down[0].astype(BF16)
    ln_g = ln_v_gain[0][None]
    ln_b = ln_v_bias[0][None]
    conv_b2 = conv_b[0][None]
    nfin = norm_final[None]

    w00 = jnp.repeat(w_spatial[0][:, 0, 0], GA_CH)[None]
    b0 = jnp.repeat(b_spatial[0][:, 0], GA_CH)[None]
    q_s, kv_s, vn_s, ma_s, sgb_s = _sample_in(x_sample[:, 0], norm_mix, w_in_b, ln_g, ln_b, w00, b0, w_pa_b)
    caches_t = [c[0].transpose(0, 2, 3, 4, 1).reshape(n_s, 2, ATT_G, c.shape[2]) for c in caches]
    sample = (q_s, kv_s, caches_t)

    mix_tile = ffn_tile = 512
    band_rows = [min(512, s // dil) for _, dil in DIL_GROUPS]
    hosts = [(b * (s // mix_tile), MIX_RIDERS_PER_STEP), (b * (s // ffn_tile), 1)] + [
        (b * (s // rows), 1) for rows in band_rows]
    riders, taken = [], 0
    for steps, per_step in hosts:
        count = min(steps * per_step, n_s - taken)
        riders.append(_Rider(taken, count, steps, per_step) if count > 0 else None)
        taken += count
    mix_rider, ffn_rider, *band_riders = riders

    mix_outs = _mix_in(
        x_prompt, norm_mix, w_in_b, ln_g, ln_b, w_spatial[0], b_spatial[0][:, :, None], w_pa_b,
        tile=mix_tile, tail=max_win, rider=mix_rider, sample=sample)
    *qkv_groups, kv_tail, ma, sgb, vn_last = mix_outs[:N_DIL + 4]
    o_list, band_parts = [], []
    for qkv_g, rows, rider in zip(qkv_groups, band_rows, band_riders):
        outs = _band_attn(qkv_g, rows=rows, rider=rider, sample=sample)
        o_list.append(outs[0])
        band_parts.extend(outs[1:])
    y_prompt, conv_prompt, *ffn_part = _merge_ffn(x_prompt, ma, sgb, o_list, w_pb_b, w_out_b, norm_ffn, w_up_b,
                                                  conv_w[0], conv_b2, w_dn_b, nfin, tile=ffn_tile,
                                                  rider=ffn_rider, sample=sample)
    b_parts = list(mix_outs[N_DIL + 4:]) + ffn_part + band_parts
    if taken < n_s:
        b_parts.append(_sample_attn(_Rider(taken, n_s - taken, n_s - taken, 1), sample))
    b_s = jnp.concatenate(b_parts, axis=0)

    kv_prompt = []
    for g, (win, _) in enumerate(DIL_GROUPS):
        k_g = kv_tail[:, max_win - win:, g * ATT_G:(g + 1) * ATT_G]
        v_g = kv_tail[:, max_win - win:, ATT_W + g * ATT_G: ATT_W + (g + 1) * ATT_G]
        kv_prompt.append(jnp.stack([k_g, v_g], axis=2).reshape(1, b, win, 2, N_HG, HEAD_DIM))

    hist = state_conv_ffn[0]
    y_s, up_s = _sample_ffn(x_sample[:, 0], ma_s, sgb_s, b_s, hist[:, 0], hist[:, 1],
                            w_pb_b, w_out_b, norm_ffn, w_up_b, conv_w[0], conv_b2, w_dn_b, nfin)

    kv_sample = [jnp.stack([kv_s[:, g * ATT_G:(g + 1) * ATT_G], kv_s[:, ATT_W + g * ATT_G: ATT_W + (g + 1) * ATT_G]],
                           axis=1).reshape(1, n_s, 1, 2, N_HG, HEAD_DIM) for g in range(N_DIL)]
    conv_sample = jnp.concatenate([hist[:, CONV_W - 2:], up_s[:, None]], axis=1)[None]

    return (y_prompt, y_s[:, None], kv_prompt[0], kv_sample[0], kv_prompt[1], kv_sample[1],
            kv_prompt[2], kv_sample[2], vn_last[None], vn_s[None, :, None], conv_prompt[None], conv_sample)
```

```python
import functools
import math
from typing import NamedTuple

import jax
import jax.numpy as jnp
from jax import lax
from jax.experimental import pallas as pl
from jax.experimental.pallas import tpu as pltpu

F32 = jnp.float32
BF16 = jnp.bfloat16

HEAD_DIM = 64
N_HG = 4
ATT_G = N_HG * HEAD_DIM
DIL_GROUPS = ((128, 1), (512, 4), (2048, 16))
N_DIL = len(DIL_GROUPS)
ATT_W = N_DIL * ATT_G
BAND = 128
CHUNK = 128
N_GA = 4
GA_CH = 128
D_A = N_GA * GA_CH
CONV_W = 3
EPS = 1e-6
GELU_C = math.sqrt(2.0 / math.pi)

VMEM_LIMIT_BYTES = 62 * 1024 * 1024
FF_BLK = 768
LANES = 128
BF16_SUBLANES = 16
MIX_RIDERS_PER_STEP = 2


def _gelu(x):
    return x * (0.5 * (1.0 + jnp.tanh(GELU_C * (x + 0.044715 * (x * x * x)))))


def _rmsnorm(x, g):
    return x * lax.rsqrt(jnp.mean(x * x, axis=-1, keepdims=True) + EPS) * g


def _layernorm(x, g, b):
    mu = jnp.mean(x, axis=-1, keepdims=True)
    xc = x - mu
    var = jnp.mean(xc * xc, axis=-1, keepdims=True)
    return xc * lax.rsqrt(var + EPS) * g + b


def _dot(a, b):
    return jnp.dot(a, b, preferred_element_type=F32)


def _dot_nt(a, b):
    return lax.dot_general(a, b, (((1,), (1,)), ((), ())), preferred_element_type=F32)


def _const_spec(shape):
    nd = len(shape)
    return pl.BlockSpec(shape, lambda *_: (0,) * nd, pipeline_mode=pl.Buffered(1))


def _params(*sem, flags=None):
    return pltpu.CompilerParams(dimension_semantics=sem, vmem_limit_bytes=VMEM_LIMIT_BYTES, flags=flags)


def _project(xn, w_ref, lo, hi, bf16_copy_ref=None):
    w = w_ref[:, lo:hi]
    if bf16_copy_ref is not None:
        w = w.astype(BF16)
        bf16_copy_ref[:, lo:hi] = w
    return _dot(xn, w)


def _gates_and_branch_a(xn, w_ref, a_bf16, wpa_ref, d_model, w_copy_ref=None, wpa_copy_ref=None):
    base = 3 * ATT_W + 2 * D_A
    pa = _project(a_bf16, wpa_ref, 0, d_model, wpa_copy_ref)
    ga = _project(xn, w_ref, base, base + d_model, w_copy_ref)
    gb = _project(xn, w_ref, base + d_model, base + 2 * d_model, w_copy_ref)
    return jax.nn.sigmoid(ga) * pa, jax.nn.sigmoid(gb)


def _combine_groups(o_list):
    lses = [o[:, ATT_G:] for o in o_list]
    m = functools.reduce(jnp.maximum, lses)
    ws = [jnp.exp(l - m) for l in lses]
    den = functools.reduce(lambda a, b: a + b, ws)
    num = functools.reduce(lambda a, b: a + b, [w * o[:, :ATT_G] for w, o in zip(ws, o_list)])
    return num / den


def _conv_ffn(hn, history_fn, wup_ref, cw_ref, cb_ref, wdn_ref, d_ff):
    acc = None
    for j0 in range(0, d_ff, FF_BLK):
        wid = min(FF_BLK, d_ff - j0)
        halves = []
        for base in (0, d_ff):
            lo = base + j0
            u = _dot(hn, wup_ref[:, lo:lo + wid])
            prev1, prev2 = history_fn(u, lo)
            c = (cb_ref[:, lo:lo + wid] + cw_ref[2:3, lo:lo + wid] * u
                 + cw_ref[0:1, lo:lo + wid] * prev2
                 + cw_ref[1:2, lo:lo + wid] * prev1)
            halves.append(c)
        act = (_gelu(halves[0]) * halves[1]).astype(BF16)
        part = _dot(act, wdn_ref[j0:j0 + wid, :])
        acc = part if acc is None else acc + part
    return acc


def _cache_attend(q_row, kv_row, c_refs):
    sub = lax.broadcasted_iota(jnp.int32, (8, ATT_G), 0)
    lane_head = lax.broadcasted_iota(jnp.int32, (8, ATT_G), 1) // HEAD_DIM
    diag = sub == lane_head
    s_g, s_new, v_new = [], [], []
    for g, (c_ref, (_, dil)) in enumerate(zip(c_refs, DIL_GROUPS)):
        q_bd = jnp.where(diag, q_row[:, g * ATT_G:(g + 1) * ATT_G], 0.0)
        s = _dot(q_bd.astype(BF16), c_ref[0, 0].astype(BF16))
        pos = lax.broadcasted_iota(jnp.int32, s.shape, 1)
        s_g.append(jnp.where(pos % dil == 0, s, -jnp.inf))
        s_new.append(jnp.sum(q_bd * kv_row[:, g * ATT_G:(g + 1) * ATT_G], axis=-1, keepdims=True))
        v_new.append(kv_row[:, ATT_W + g * ATT_G:ATT_W + (g + 1) * ATT_G])
    s_all = jnp.concatenate(s_g, axis=1)
    m = functools.reduce(jnp.maximum, s_new + [jnp.max(s_all, axis=-1, keepdims=True)])
    p = jnp.exp(s_all - m)
    p_new = [jnp.exp(sn - m) for sn in s_new]
    den = functools.reduce(lambda a, b: a + b, p_new + [jnp.sum(p, axis=-1, keepdims=True)])
    num, lo = None, 0
    for g, c_ref in enumerate(c_refs):
        hi = lo + c_ref.shape[-1]
        part = _dot_nt(p[:, lo:hi].astype(BF16), c_ref[0, 1].astype(BF16)) + p_new[g] * v_new[g]
        num = part if num is None else num + part
        lo = hi
    return jnp.sum(jnp.where(diag, num / den, 0.0), axis=0, keepdims=True)


class _Rider(NamedTuple):
    offset: int
    count: int
    steps: int
    per_step: int

    @property
    def n_in(self):
        return 2 + N_DIL * self.per_step


def _rider_operands(rider, lin_fn, q_s, kv_s, caches_t):
    n = q_s.shape[0]

    def cache_map(k, *ids):
        return (jnp.minimum(rider.offset + lin_fn(*ids) * rider.per_step + k, n - 1), 0, 0, 0)

    in_specs = [_const_spec(q_s.shape), _const_spec(kv_s.shape)]
    args = [q_s, kv_s]
    for k in range(rider.per_step):
        in_specs += [pl.BlockSpec((1,) + c.shape[1:], functools.partial(cache_map, k)) for c in caches_t]
        args += caches_t
    out_spec = pl.BlockSpec((rider.count, ATT_G), lambda *ids: (0, 0))
    out_shape = jax.ShapeDtypeStruct((rider.count, ATT_G), F32)
    return in_specs, out_spec, out_shape, args


def _ride(rider, lin, qs_ref, kvs_ref, c_refs, os_ref):
    for k in range(rider.per_step):
        local = lin * rider.per_step + k

        def work(local=local, k=k):
            seq = rider.offset + local
            os_ref[pl.ds(local, 1), :] = _cache_attend(qs_ref[pl.ds(seq, 1), :], kvs_ref[pl.ds(seq, 1), :],
                                                       c_refs[k * N_DIL:(k + 1) * N_DIL])

        if rider.count == rider.steps * rider.per_step:
            work()
        else:
            pl.when(local < rider.count)(work)


def _mix_in_kernel(x_ref, g_ref, w_ref, lng_ref, lnb_ref, wsp_ref, bsp_ref, wpa_ref, *rest, d_model, rider):
    if rider is not None:
        qs_ref, kvs_ref, *c_refs = rest[:rider.n_in]
        rest = rest[rider.n_in:]
        *qkv_refs, ma_ref, sgb_ref, vn_ref, os_ref, slab_ref = rest
        _ride(rider, pl.program_id(0) * pl.num_programs(1) + pl.program_id(1), qs_ref, kvs_ref, c_refs, os_ref)
    else:
        *qkv_refs, ma_ref, sgb_ref, vn_ref, slab_ref = rest
    qkv0_ref, qkv1_ref, qkv2_ref, *kvt_refs = qkv_refs
    t = x_ref.shape[1]
    xn = _rmsnorm(x_ref[0], g_ref[...]).astype(BF16)
    q = _project(xn, w_ref, 0, ATT_W) * (HEAD_DIM ** -0.5)
    kv = _project(xn, w_ref, ATT_W, 3 * ATT_W)
    for g, kvt_ref in enumerate(kvt_refs):
        tail_rows = slice(t - kvt_ref.shape[-1], t)
        kvt_ref[0, 0] = kv[tail_rows, g * ATT_G:(g + 1) * ATT_G].T
        kvt_ref[0, 1] = kv[tail_rows, ATT_W + g * ATT_G:ATT_W + (g + 1) * ATT_G].T
    per_slab = ATT_G // LANES
    for g, ((_, dil), out_ref) in enumerate(zip(DIL_GROUPS, (qkv0_ref, qkv1_ref, qkv2_ref))):
        secs = [q[:, g * ATT_G:(g + 1) * ATT_G], kv[:, g * ATT_G:(g + 1) * ATT_G],
                kv[:, ATT_W + g * ATT_G:ATT_W + (g + 1) * ATT_G]]
        if dil == 1:
            out_ref[0, 0] = jnp.concatenate(secs, axis=1).astype(BF16)
            continue
        for i, sec in enumerate(secs):
            for k in range(per_slab):
                slab_ref[i * per_slab + k] = sec[:, k * LANES:(k + 1) * LANES]
        for r in range(dil):
            pieces = [slab_ref[n, pl.ds(r, t // dil, stride=dil), :] for n in range(3 * per_slab)]
            out_ref[0, r] = jnp.concatenate(pieces, axis=1).astype(BF16)

    u = _project(xn, w_ref, 3 * ATT_W, 3 * ATT_W + D_A)
    va = _project(xn, w_ref, 3 * ATT_W + D_A, 3 * ATT_W + 2 * D_A)
    vn = _layernorm(_gelu(va), lng_ref[...], lnb_ref[...])
    vn_ref[0] = vn[t - CHUNK:, :]
    vnb = vn.astype(BF16)
    row = lax.broadcasted_iota(jnp.int32, (CHUNK, CHUNK), 0)
    col = lax.broadcasted_iota(jnp.int32, (CHUNK, CHUNK), 1)
    w_m = [jnp.where(row >= col, wsp_ref[g], 0.0).astype(BF16) for g in range(N_GA)]
    rows = []
    for c in range(t // CHUNK):
        parts = [_dot(w_m[g], vnb[c * CHUNK:(c + 1) * CHUNK, g * GA_CH:(g + 1) * GA_CH]) + bsp_ref[g]
                 for g in range(N_GA)]
        rows.append(jnp.concatenate(parts, axis=1))
    mixed = jnp.concatenate(rows, axis=0)
    a = (_gelu(u) * mixed).astype(BF16)
    ma, sgb = _gates_and_branch_a(xn, w_ref, a, wpa_ref, d_model)
    ma_ref[0] = ma.astype(BF16)
    sgb_ref[0] = sgb.astype(BF16)


def _cast_chunk_rows(rows, steps):
    return next(c for c in range(BF16_SUBLANES, rows + 1, BF16_SUBLANES) if rows % c == 0 and rows // c <= steps)


def _mix_in(x, norm_mix, w_in, ln_g, ln_b, w_sp, b_sp, w_pa, *, tile, rider, sample):
    b, s, d = x.shape
    n_t = s // tile
    in_cols = w_in.shape[1]

    def tail_spec(win):
        w_blk = min(win, tile)
        first = n_t - win // w_blk
        return pl.BlockSpec((1, 2, ATT_G, w_blk), lambda i, j: (i, 0, 0, jnp.maximum(j - first, 0)))

    kern = functools.partial(_mix_in_kernel, d_model=d, rider=rider)
    in_specs = [
        pl.BlockSpec((1, tile, d), lambda i, j: (i, j, 0)),
        _const_spec((1, d)),
        _const_spec((d, in_cols)),
        _const_spec((1, D_A)),
        _const_spec((1, D_A)),
        _const_spec((N_GA, CHUNK, CHUNK)),
        _const_spec((N_GA, CHUNK, 1)),
        _const_spec((D_A, d)),
    ]
    out_specs = [
        pl.BlockSpec((1, dil, tile // dil, 3 * ATT_G), lambda i, j: (i, 0, j, 0)) for _, dil in DIL_GROUPS
    ] + [tail_spec(win) for win, _ in DIL_GROUPS] + [
        pl.BlockSpec((1, tile, d), lambda i, j: (i, j, 0)),
        pl.BlockSpec((1, tile, d), lambda i, j: (i, j, 0)),
        pl.BlockSpec((1, CHUNK, D_A), lambda i, j: (i, 0, 0)),
    ]
    out_shape = [
        jax.ShapeDtypeStruct((b, dil, s // dil, 3 * ATT_G), BF16) for _, dil in DIL_GROUPS
    ] + [jax.ShapeDtypeStruct((b, 2, ATT_G, win), F32) for win, _ in DIL_GROUPS] + [
        jax.ShapeDtypeStruct((b, s, d), BF16),
        jax.ShapeDtypeStruct((b, s, d), BF16),
        jax.ShapeDtypeStruct((b, CHUNK, D_A), F32),
    ]
    args = [x, norm_mix, w_in, ln_g, ln_b, w_sp, b_sp, w_pa]
    if rider is not None:
        r_in, r_out, r_shape, r_args = _rider_operands(rider, lambda i, j: i * n_t + j, *sample)
        in_specs, out_specs, out_shape, args = in_specs + r_in, out_specs + [r_out], out_shape + [r_shape], args + r_args
    return pl.pallas_call(
        kern,
        grid=(b, n_t),
        in_specs=in_specs,
        out_specs=out_specs,
        out_shape=out_shape,
        scratch_shapes=[pltpu.VMEM((3 * ATT_G // LANES, tile, LANES), F32)],
        compiler_params=_params("arbitrary", "arbitrary"),
        name="prompt_mix_in",
    )(*args)


def _band_attn_kernel(cur_ref, halo_ref, *rest, rider, n_cast):
    cast_in, rest = rest[:n_cast], rest[n_cast:]
    if rider is not None:
        qs_ref, kvs_ref, *c_refs = rest[:rider.n_in]
        rest = rest[rider.n_in:]
        o_ref, *cast_out, os_ref = rest
        lin = (pl.program_id(0) * pl.num_programs(1) + pl.program_id(1)) * pl.num_programs(2) + pl.program_id(2)
        _ride(rider, lin, qs_ref, kvs_ref, c_refs, os_ref)
    else:
        o_ref, *cast_out = rest
    for src, dst in zip(cast_in, cast_out):
        dst[...] = src[...].astype(BF16)
    rows = cur_ref.shape[2]
    first = pl.program_id(2) == 0
    cur = cur_ref[0, 0]
    halo = halo_ref[0, 0]
    q_all = cur[:, 0:ATT_G]
    k_all = jnp.concatenate([halo[:, ATT_G:2 * ATT_G], cur[:, ATT_G:2 * ATT_G]], axis=0)
    v_all = jnp.concatenate([halo[:, 2 * ATT_G:], cur[:, 2 * ATT_G:]], axis=0)
    lane_head = lax.broadcasted_iota(jnp.int32, (1, ATT_G), 1) // HEAD_DIM
    zero = jnp.zeros((), BF16)
    q_h = [jnp.where(lane_head == h, q_all, zero) for h in range(N_HG)]
    qi = lax.broadcasted_iota(jnp.int32, (N_HG * BAND, 2 * BAND), 0) % BAND
    ki = lax.broadcasted_iota(jnp.int32, (N_HG * BAND, 2 * BAND), 1)
    band = jnp.logical_and(ki >= qi, ki <= qi + BAND)
    band_first = jnp.logical_and(band, jnp.logical_or(ki >= BAND, jnp.logical_not(first)))
    n_qb = rows // BAND
    tiles = []
    for qb in range(n_qb):
        lhs = jnp.concatenate([q[qb * BAND:(qb + 1) * BAND] for q in q_h], axis=0)
        s = _dot_nt(lhs, k_all[qb * BAND:(qb + 2) * BAND])
        tiles.append(jnp.where(band_first if qb == 0 else band, s, -jnp.inf))
    s_all = jnp.concatenate(tiles, axis=0)
    m = jnp.max(s_all, axis=-1, keepdims=True)
    p = jnp.exp(s_all - m)
    l = jnp.sum(p, axis=-1, keepdims=True)
    p_n = (p * (1.0 / l)).astype(BF16)
    lse = m + jnp.log(l)
    for qb in range(n_qb):
        u0 = qb * N_HG * BAND
        pv = _dot(p_n[u0:u0 + N_HG * BAND], v_all[qb * BAND:(qb + 2) * BAND])
        o = pv[(N_HG - 1) * BAND:]
        lse_row = jnp.broadcast_to(lse[u0 + (N_HG - 1) * BAND:u0 + N_HG * BAND], (BAND, ATT_G))
        for h in range(N_HG - 2, -1, -1):
            o = jnp.where(lane_head == h, pv[h * BAND:(h + 1) * BAND], o)
            lse_row = jnp.where(lane_head == h, lse[u0 + h * BAND:u0 + (h + 1) * BAND], lse_row)
        o_ref[0, 0, qb * BAND:(qb + 1) * BAND, :] = jnp.concatenate([o, lse_row], axis=1)


def _band_attn(qkv_g, cast_ws, *, rows, rider, sample):
    b, d, m, w = qkv_g.shape
    per = rows // BAND
    n_j = m // rows
    lin = lambda i, r, j: (i * d + r) * n_j + j

    def cast_spec(wt):
        n_rows, cols = wt.shape
        c = _cast_chunk_rows(n_rows, b * d * n_j)
        return pl.BlockSpec((c, cols), lambda i, r, j: (jnp.minimum(lin(i, r, j), n_rows // c - 1), 0))

    in_specs = [
        pl.BlockSpec((1, 1, rows, w), lambda i, r, j: (i, r, j, 0)),
        pl.BlockSpec((1, 1, BAND, w), lambda i, r, j: (i, r, jnp.maximum(j * per - 1, 0), 0)),
    ] + [cast_spec(wt) for wt in cast_ws]
    out_specs = [pl.BlockSpec((1, 1, rows, 2 * ATT_G), lambda i, r, j: (i, r, j, 0))] + [cast_spec(wt) for wt in cast_ws]
    out_shape = [jax.ShapeDtypeStruct((b, d, m, 2 * ATT_G), F32)] + [jax.ShapeDtypeStruct(wt.shape, BF16) for wt in cast_ws]
    args = [qkv_g, qkv_g, *cast_ws]
    if rider is not None:
        r_in, r_out, r_shape, r_args = _rider_operands(rider, lin, *sample)
        in_specs, out_specs, out_shape, args = in_specs + r_in, out_specs + [r_out], out_shape + [r_shape], args + r_args
    return pl.pallas_call(
        functools.partial(_band_attn_kernel, rider=rider, n_cast=len(cast_ws)),
        grid=(b, d, n_j),
        in_specs=in_specs,
        out_specs=out_specs,
        out_shape=out_shape,
        compiler_params=_params("arbitrary", "arbitrary", "arbitrary"),
        name=f"prompt_band_attn_d{d}",
    )(*args)


def _merge_ffn_kernel(x_ref, ma_ref, sgb_ref, o0_ref, o1_ref, o2_ref, wpb_ref, wout_ref, nf_ref,
                      wup_ref, cw_ref, cb_ref, wdn_ref, nfin_ref, *rest, d_ff, rider):
    if rider is not None:
        qs_ref, kvs_ref, *c_refs = rest[:rider.n_in]
        y_ref, ct_ref, os_ref, carry_ref, hist_ref, tok_ref = rest[rider.n_in:]
    else:
        y_ref, ct_ref, carry_ref, hist_ref, tok_ref = rest
    t = x_ref.shape[1]

    @pl.when(pl.program_id(1) == 0)
    def _():
        carry_ref[...] = jnp.zeros_like(carry_ref)

    if rider is not None:
        _ride(rider, pl.program_id(0) * pl.num_programs(1) + pl.program_id(1), qs_ref, kvs_ref, c_refs, os_ref)

    n_slab = 2 * ATT_G // LANES
    o_list = []
    for (_, dil), o_ref in zip(DIL_GROUPS, (o0_ref, o1_ref, o2_ref)):
        if dil == 1:
            o_list.append(o_ref[0, 0])
            continue
        for r in range(dil):
            for k in range(n_slab):
                tok_ref[k, pl.ds(r, t // dil, stride=dil), :] = o_ref[0, r, :, k * LANES:(k + 1) * LANES]
        o_list.append(jnp.concatenate([tok_ref[k] for k in range(n_slab)], axis=1))
    b_out = _combine_groups(o_list).astype(BF16)
    merged = ma_ref[0].astype(F32) + sgb_ref[0].astype(F32) * _dot(b_out, wpb_ref[...])
    h = x_ref[0] + _dot(merged.astype(BF16), wout_ref[...])
    hn = _rmsnorm(h, nf_ref[...]).astype(BF16)

    def history(u, lo):
        prev1, prev2 = [], []
        for k in range(u.shape[1] // LANES):
            cols = slice(lo + k * LANES, lo + (k + 1) * LANES)
            hist_ref[k, pl.ds(0, 8, stride=2), :] = carry_ref[:, cols]
            hist_ref[k, pl.ds(16, t, stride=2), :] = u[:, k * LANES:(k + 1) * LANES]
            carry_ref[:, cols] = u[t - 8:, k * LANES:(k + 1) * LANES]
            prev1.append(hist_ref[k, pl.ds(14, t, stride=2), :])
            prev2.append(hist_ref[k, pl.ds(12, t, stride=2), :])
        return jnp.concatenate(prev1, axis=1), jnp.concatenate(prev2, axis=1)

    ffn = _conv_ffn(hn, history, wup_ref, cw_ref, cb_ref, wdn_ref, d_ff)
    ct_ref[0] = carry_ref[8 - (CONV_W - 1):, :]
    y_ref[0] = _rmsnorm(h + ffn, nfin_ref[...])


def _merge_ffn(x, ma, sgb, o_list, w_pb, w_out, norm_ffn, w_up, conv_w, conv_b, w_dn, norm_final, *, tile,
               rider, sample):
    b, s, d = x.shape
    d_ff = w_dn.shape[0]
    n_t = s // tile
    tok = lambda w: pl.BlockSpec((1, tile, w), lambda i, j: (i, j, 0))
    kern = functools.partial(_merge_ffn_kernel, d_ff=d_ff, rider=rider)
    in_specs = [tok(d), tok(d), tok(d)] + [
        pl.BlockSpec((1, dil, tile // dil, 2 * ATT_G), lambda i, j: (i, 0, j, 0)) for _, dil in DIL_GROUPS
    ] + [
        _const_spec((ATT_G, d)), _const_spec((d, d)), _const_spec((1, d)),
        _const_spec((d, 2 * d_ff)), _const_spec((CONV_W, 2 * d_ff)), _const_spec((1, 2 * d_ff)),
        _const_spec((d_ff, d)), _const_spec((1, d))]
    out_specs = [tok(d), pl.BlockSpec((1, CONV_W - 1, 2 * d_ff), lambda i, j: (i, 0, 0))]
    out_shape = [jax.ShapeDtypeStruct((b, s, d), F32), jax.ShapeDtypeStruct((b, CONV_W - 1, 2 * d_ff), F32)]
    args = [x, ma, sgb, *o_list, w_pb, w_out, norm_ffn, w_up, conv_w, conv_b, w_dn, norm_final]
    if rider is not None:
        r_in, r_out, r_shape, r_args = _rider_operands(rider, lambda i, j: i * n_t + j, *sample)
        in_specs, out_specs, out_shape, args = in_specs + r_in, out_specs + [r_out], out_shape + [r_shape], args + r_args
    return pl.pallas_call(
        kern,
        grid=(b, n_t),
        in_specs=in_specs,
        out_specs=out_specs,
        out_shape=out_shape,
        scratch_shapes=[pltpu.VMEM((8, 2 * d_ff), F32),
                        pltpu.VMEM((FF_BLK // LANES, 2 * (tile + 8), LANES), F32),
                        pltpu.VMEM((2 * ATT_G // LANES, tile, LANES), F32)],
        compiler_params=_params("arbitrary", "arbitrary"),
        name="prompt_merge_ffn",
    )(*args)


def _sample_in_kernel(x_ref, g_ref, w_ref, lng_ref, lnb_ref, w00_ref, b0_ref, wpa_ref,
                      q_ref, kv_ref, vn_ref, ma_ref, sgb_ref, wb_ref, wpab_ref, *, d_model):
    xn = _rmsnorm(x_ref[...], g_ref[...]).astype(BF16)
    q_ref[...] = _project(xn, w_ref, 0, ATT_W, wb_ref) * (HEAD_DIM ** -0.5)
    kv_ref[...] = _project(xn, w_ref, ATT_W, 3 * ATT_W, wb_ref)
    u = _project(xn, w_ref, 3 * ATT_W, 3 * ATT_W + D_A, wb_ref)
    va = _project(xn, w_ref, 3 * ATT_W + D_A, 3 * ATT_W + 2 * D_A, wb_ref)
    vn = _layernorm(_gelu(va), lng_ref[...], lnb_ref[...])
    vn_ref[...] = vn
    mixed = w00_ref[...] * vn + b0_ref[...]
    a = (_gelu(u) * mixed).astype(BF16)
    ma, sgb = _gates_and_branch_a(xn, w_ref, a, wpa_ref, d_model, wb_ref, wpab_ref)
    ma_ref[...] = ma
    sgb_ref[...] = sgb


def _sample_in(x, norm_mix, w_in, ln_g, ln_b, w00, b0, w_pa):
    n, d = x.shape
    in_cols = w_in.shape[1]
    kern = functools.partial(_sample_in_kernel, d_model=d)
    full = lambda *shape: pl.BlockSpec(shape, lambda: (0,) * len(shape))
    return pl.pallas_call(
        kern,
        in_specs=[full(n, d), full(1, d), full(d, in_cols), full(1, D_A), full(1, D_A),
                  full(1, D_A), full(1, D_A), full(D_A, d)],
        out_specs=[full(n, ATT_W), full(n, 2 * ATT_W), full(n, D_A), full(n, d), full(n, d),
                   full(d, in_cols), full(D_A, d)],
        out_shape=[jax.ShapeDtypeStruct((n, ATT_W), F32), jax.ShapeDtypeStruct((n, 2 * ATT_W), F32),
                   jax.ShapeDtypeStruct((n, D_A), F32), jax.ShapeDtypeStruct((n, d), F32),
                   jax.ShapeDtypeStruct((n, d), F32),
                   jax.ShapeDtypeStruct((d, in_cols), BF16), jax.ShapeDtypeStruct((D_A, d), BF16)],
        compiler_params=pltpu.CompilerParams(vmem_limit_bytes=VMEM_LIMIT_BYTES),
        name="sample_in",
    )(x, norm_mix, w_in, ln_g, ln_b, w00, b0, w_pa)


def _sample_attn_kernel(qs_ref, kvs_ref, *rest, rider):
    *c_refs, os_ref = rest
    _ride(rider, pl.program_id(0), qs_ref, kvs_ref, c_refs, os_ref)


def _sample_attn(rider, sample):
    r_in, r_out, r_shape, r_args = _rider_operands(rider, lambda i: i, *sample)
    return pl.pallas_call(
        functools.partial(_sample_attn_kernel, rider=rider),
        grid=(rider.steps,),
        in_specs=r_in,
        out_specs=r_out,
        out_shape=r_shape,
        compiler_params=_params("arbitrary"),
        name="sample_cache_attn",
    )(*r_args)


def _sample_ffn_kernel(x_ref, ma_ref, sgb_ref, b_ref, h0_ref, h1_ref, wpb_ref, wout_ref, nf_ref,
                       wup_ref, cw_ref, cb_ref, wdn_ref, nfin_ref, y_ref, up_ref, *, d_ff):
    merged = ma_ref[...] + sgb_ref[...] * _dot(b_ref[...].astype(BF16), wpb_ref[...])
    h = x_ref[...] + _dot(merged.astype(BF16), wout_ref[...])
    hn = _rmsnorm(h, nf_ref[...]).astype(BF16)

    def history(u, lo):
        up_ref[:, lo:lo + u.shape[1]] = u
        return h1_ref[:, lo:lo + u.shape[1]], h0_ref[:, lo:lo + u.shape[1]]

    ffn = _conv_ffn(hn, history, wup_ref, cw_ref, cb_ref, wdn_ref, d_ff)
    y_ref[...] = _rmsnorm(h + ffn, nfin_ref[...])


def _sample_ffn(x, ma, sgb, b_out, hist0, hist1, w_pb, w_out, norm_ffn, w_up, conv_w, conv_b, w_dn, norm_final):
    n, d = x.shape
    d_ff = w_dn.shape[0]
    kern = functools.partial(_sample_ffn_kernel, d_ff=d_ff)
    full = lambda *shape: pl.BlockSpec(shape, lambda: (0,) * len(shape))
    return pl.pallas_call(
        kern,
        in_specs=[full(n, d), full(n, d), full(n, d), full(n, ATT_G), full(n, 2 * d_ff), full(n, 2 * d_ff),
                  full(ATT_G, d), full(d, d), full(1, d), full(d, 2 * d_ff), full(CONV_W, 2 * d_ff),
                  full(1, 2 * d_ff), full(d_ff, d), full(1, d)],
        out_specs=[full(n, d), full(n, 2 * d_ff)],
        out_shape=[jax.ShapeDtypeStruct((n, d), F32), jax.ShapeDtypeStruct((n, 2 * d_ff), F32)],
        compiler_params=pltpu.CompilerParams(vmem_limit_bytes=VMEM_LIMIT_BYTES),
        name="sample_merge_ffn",
    )(x, ma, sgb, b_out, hist0, hist1, w_pb, w_out, norm_ffn, w_up, conv_w, conv_b, w_dn, norm_final)


def kernel(x_prompt, x_sample, cache_kv_w128, cache_kv_w512, cache_kv_w2048, state_conv_ffn, norm_mix, w_in, ln_v_gain, ln_v_bias, w_spatial, b_spatial, w_proj_a, w_proj_b, w_out, norm_ffn, w_up, conv_w, conv_b, w_down, norm_final):
    b, s, d = x_prompt.shape
    n_s, t_s, _ = x_sample.shape
    depth = w_in.shape[0]
    caches = (cache_kv_w128, cache_kv_w512, cache_kv_w2048)
    max_win = max(w for w, _ in DIL_GROUPS)
    assert depth == 1 and t_s == 1, "one layer, one new token per sample sequence"
    assert s % max_win == 0, "prompt length must cover whole strided query blocks of every dilation group"
    for (win, _), c in zip(DIL_GROUPS, caches):
        assert c.shape[2] == win, "cache must hold the whole window"

    ln_g = ln_v_gain[0][None]
    ln_b = ln_v_bias[0][None]
    conv_b2 = conv_b[0][None]
    nfin = norm_final[None]

    w00 = jnp.repeat(w_spatial[0][:, 0, 0], GA_CH)[None]
    b0 = jnp.repeat(b_spatial[0][:, 0], GA_CH)[None]
    q_s, kv_s, vn_s, ma_s, sgb_s, w_in_b, w_pa_b = _sample_in(
        x_sample[:, 0], norm_mix, w_in[0], ln_g, ln_b, w00, b0, w_proj_a[0])
    caches_t = [c[0].transpose(0, 2, 3, 4, 1).reshape(n_s, 2, ATT_G, c.shape[2]) for c in caches]
    sample = (q_s, kv_s, caches_t)

    mix_tile = ffn_tile = 512
    band_rows = [min(512, s // dil) for _, dil in DIL_GROUPS]
    hosts = [(b * (s // mix_tile), MIX_RIDERS_PER_STEP), (b * (s // ffn_tile), 1)] + [
        (b * (s // rows), 1) for rows in band_rows]
    riders, taken = [], 0
    for steps, per_step in hosts:
        count = min(steps * per_step, n_s - taken)
        riders.append(_Rider(taken, count, steps, per_step) if count > 0 else None)
        taken += count
    mix_rider, ffn_rider, *band_riders = riders

    mix_outs = _mix_in(
        x_prompt, norm_mix, w_in_b, ln_g, ln_b, w_spatial[0], b_spatial[0][:, :, None], w_pa_b,
        tile=mix_tile, rider=mix_rider, sample=sample)
    qkv_groups, kv_tails = mix_outs[:N_DIL], mix_outs[N_DIL:2 * N_DIL]
    ma, sgb, vn_last, *mix_part = mix_outs[2 * N_DIL:]
    ffn_ws = [w_proj_b[0], w_out[0], w_up[0], w_down[0]]
    o_list, band_parts = [], []
    for g, (qkv_g, rows, rider) in enumerate(zip(qkv_groups, band_rows, band_riders)):
        cast_ws = ffn_ws if g == N_DIL - 1 else []
        o_g, *extra = _band_attn(qkv_g, cast_ws, rows=rows, rider=rider, sample=sample)
        o_list.append(o_g)
        if cast_ws:
            (w_pb_b, w_out_b, w_up_b, w_dn_b), extra = extra[:len(cast_ws)], extra[len(cast_ws):]
        band_parts.extend(extra)
    y_prompt, conv_prompt, *ffn_part = _merge_ffn(x_prompt, ma, sgb, o_list, w_pb_b, w_out_b, norm_ffn, w_up_b,
                                                  conv_w[0], conv_b2, w_dn_b, nfin, tile=ffn_tile,
                                                  rider=ffn_rider, sample=sample)
    b_parts = mix_part + ffn_part + band_parts
    if taken < n_s:
        b_parts.append(_sample_attn(_Rider(taken, n_s - taken, n_s - taken, 1), sample))
    b_s = jnp.concatenate(b_parts, axis=0)

    kv_prompt = [t.reshape(b, 2, N_HG, HEAD_DIM, t.shape[-1]).transpose(0, 4, 1, 2, 3)[None] for t in kv_tails]

    hist = state_conv_ffn[0]
    y_s, up_s = _sample_ffn(x_sample[:, 0], ma_s, sgb_s, b_s, hist[:, 0], hist[:, 1],
                            w_pb_b, w_out_b, norm_ffn, w_up_b, conv_w[0], conv_b2, w_dn_b, nfin)

    kv_sample = [jnp.stack([kv_s[:, g * ATT_G:(g + 1) * ATT_G], kv_s[:, ATT_W + g * ATT_G: ATT_W + (g + 1) * ATT_G]],
                           axis=1).reshape(1, n_s, 1, 2, N_HG, HEAD_DIM) for g in range(N_DIL)]
    conv_sample = jnp.concatenate([hist[:, CONV_W - 2:], up_s[:, None]], axis=1)[None]

    return (y_prompt, y_s[:, None], kv_prompt[0], kv_sample[0], kv_prompt[1], kv_sample[1],
            kv_prompt[2], kv_sample[2], vn_last[None], vn_s[None, :, None], conv_prompt[None], conv_sample)
```

```python
import functools
import math
from typing import NamedTuple

import jax
import jax.numpy as jnp
from jax import lax
from jax.experimental import pallas as pl
from jax.experimental.pallas import tpu as pltpu

F32 = jnp.float32
BF16 = jnp.bfloat16

HEAD_DIM = 64
N_HG = 4
ATT_G = N_HG * HEAD_DIM
DIL_GROUPS = ((128, 1), (512, 4), (2048, 16))
N_DIL = len(DIL_GROUPS)
ATT_W = N_DIL * ATT_G
BAND = 128
CHUNK = 128
N_GA = 4
GA_CH = 128
D_A = N_GA * GA_CH
CONV_W = 3
EPS = 1e-6
GELU_C = math.sqrt(2.0 / math.pi)

VMEM_LIMIT_BYTES = 62 * 1024 * 1024
FF_BLK = 768
LANES = 128
BF16_SUBLANES = 16
MIX_RIDERS_PER_STEP = 2


def _gelu(x):
    return x * (0.5 * (1.0 + jnp.tanh(GELU_C * (x + 0.044715 * (x * x * x)))))


def _rmsnorm(x, g):
    return x * lax.rsqrt(jnp.mean(x * x, axis=-1, keepdims=True) + EPS) * g


def _layernorm(x, g, b):
    mu = jnp.mean(x, axis=-1, keepdims=True)
    xc = x - mu
    var = jnp.mean(xc * xc, axis=-1, keepdims=True)
    return xc * lax.rsqrt(var + EPS) * g + b


def _dot(a, b):
    return jnp.dot(a, b, preferred_element_type=F32)


def _dot_nt(a, b):
    return lax.dot_general(a, b, (((1,), (1,)), ((), ())), preferred_element_type=F32)


def _const_spec(shape):
    nd = len(shape)
    return pl.BlockSpec(shape, lambda *_: (0,) * nd, pipeline_mode=pl.Buffered(1))


def _params(*sem, flags=None):
    return pltpu.CompilerParams(dimension_semantics=sem, vmem_limit_bytes=VMEM_LIMIT_BYTES, flags=flags)


def _project(xn, w_ref, lo, hi, bf16_copy_ref=None):
    w = w_ref[:, lo:hi]
    if bf16_copy_ref is not None:
        w = w.astype(BF16)
        bf16_copy_ref[:, lo:hi] = w
    return _dot(xn, w)


def _gates_and_branch_a(xn, w_ref, a_bf16, wpa_ref, d_model, w_copy_ref=None, wpa_copy_ref=None):
    base = 3 * ATT_W + 2 * D_A
    pa = _project(a_bf16, wpa_ref, 0, d_model, wpa_copy_ref)
    ga = _project(xn, w_ref, base, base + d_model, w_copy_ref)
    gb = _project(xn, w_ref, base + d_model, base + 2 * d_model, w_copy_ref)
    return jax.nn.sigmoid(ga) * pa, jax.nn.sigmoid(gb)


def _combine_groups(o_list):
    lses = [o[:, ATT_G:] for o in o_list]
    m = functools.reduce(jnp.maximum, lses)
    ws = [jnp.exp(l - m) for l in lses]
    den = functools.reduce(lambda a, b: a + b, ws)
    num = functools.reduce(lambda a, b: a + b, [w * o[:, :ATT_G] for w, o in zip(ws, o_list)])
    return num / den


def _conv_ffn(hn, history_fn, wup_ref, cw_ref, cb_ref, wdn_ref, d_ff):
    acc = None
    for j0 in range(0, d_ff, FF_BLK):
        wid = min(FF_BLK, d_ff - j0)
        halves = []
        for base in (0, d_ff):
            lo = base + j0
            u = _dot(hn, wup_ref[:, lo:lo + wid])
            prev1, prev2 = history_fn(u, lo)
            c = (cb_ref[:, lo:lo + wid] + cw_ref[2:3, lo:lo + wid] * u
                 + cw_ref[0:1, lo:lo + wid] * prev2
                 + cw_ref[1:2, lo:lo + wid] * prev1)
            halves.append(c)
        act = (_gelu(halves[0]) * halves[1]).astype(BF16)
        part = _dot(act, wdn_ref[j0:j0 + wid, :])
        acc = part if acc is None else acc + part
    return acc


def _cache_attend(q_row, kv_row, c_refs):
    sub = lax.broadcasted_iota(jnp.int32, (8, ATT_G), 0)
    lane_head = lax.broadcasted_iota(jnp.int32, (8, ATT_G), 1) // HEAD_DIM
    diag = sub == lane_head
    s_g, s_new, v_new = [], [], []
    for g, (c_ref, (_, dil)) in enumerate(zip(c_refs, DIL_GROUPS)):
        q_bd = jnp.where(diag, q_row[:, g * ATT_G:(g + 1) * ATT_G], 0.0)
        s = _dot(q_bd.astype(BF16), c_ref[0, 0].astype(BF16))
        pos = lax.broadcasted_iota(jnp.int32, s.shape, 1)
        s_g.append(jnp.where(pos % dil == 0, s, -jnp.inf))
        s_new.append(jnp.sum(q_bd * kv_row[:, g * ATT_G:(g + 1) * ATT_G], axis=-1, keepdims=True))
        v_new.append(kv_row[:, ATT_W + g * ATT_G:ATT_W + (g + 1) * ATT_G])
    s_all = jnp.concatenate(s_g, axis=1)
    m = functools.reduce(jnp.maximum, s_new + [jnp.max(s_all, axis=-1, keepdims=True)])
    p = jnp.exp(s_all - m)
    p_new = [jnp.exp(sn - m) for sn in s_new]
    den = functools.reduce(lambda a, b: a + b, p_new + [jnp.sum(p, axis=-1, keepdims=True)])
    cols = []
    for h in range(N_HG):
        acc, lo = None, 0
        for c_ref in c_refs:
            for c in range(c_ref.shape[-1] // LANES):
                term = (c_ref[0, 1, h * HEAD_DIM:(h + 1) * HEAD_DIM, c * LANES:(c + 1) * LANES]
                        * p[h:h + 1, lo:lo + LANES])
                acc = term if acc is None else acc + term
                lo += LANES
        cols.append(jnp.sum(acc, axis=1, keepdims=True))
    cached = jnp.broadcast_to(jnp.concatenate(cols, axis=0), (ATT_G, LANES)).T[0:1, :]
    new = functools.reduce(lambda a, b: a + b, [pn * vn for pn, vn in zip(p_new, v_new)])
    new_row = jnp.sum(jnp.where(diag, new, 0.0), axis=0, keepdims=True)
    den_row = jnp.sum(jnp.where(diag, den, 0.0), axis=0, keepdims=True)
    return (cached + new_row) / den_row


class _Rider(NamedTuple):
    offset: int
    count: int
    steps: int
    per_step: int

    @property
    def n_in(self):
        return 2 + N_DIL * self.per_step


def _rider_operands(rider, lin_fn, q_s, kv_s, caches_t):
    n = q_s.shape[0]

    def cache_map(k, *ids):
        return (jnp.minimum(rider.offset + lin_fn(*ids) * rider.per_step + k, n - 1), 0, 0, 0)

    in_specs = [_const_spec(q_s.shape), _const_spec(kv_s.shape)]
    args = [q_s, kv_s]
    for k in range(rider.per_step):
        in_specs += [pl.BlockSpec((1,) + c.shape[1:], functools.partial(cache_map, k)) for c in caches_t]
        args += caches_t
    out_spec = pl.BlockSpec((rider.count, ATT_G), lambda *ids: (0, 0))
    out_shape = jax.ShapeDtypeStruct((rider.count, ATT_G), F32)
    return in_specs, out_spec, out_shape, args


def _ride(rider, lin, qs_ref, kvs_ref, c_refs, os_ref, which=None):
    for k in range(rider.per_step) if which is None else which:
        if k >= rider.per_step:
            continue
        local = lin * rider.per_step + k

        def work(local=local, k=k):
            seq = rider.offset + local
            os_ref[pl.ds(local, 1), :] = _cache_attend(qs_ref[pl.ds(seq, 1), :], kvs_ref[pl.ds(seq, 1), :],
                                                       c_refs[k * N_DIL:(k + 1) * N_DIL])

        if rider.count == rider.steps * rider.per_step:
            work()
        else:
            pl.when(local < rider.count)(work)


def _mix_in_kernel(x_ref, g_ref, w_ref, lng_ref, lnb_ref, wsp_ref, bsp_ref, wpa_ref, *rest, d_model, rider):
    if rider is not None:
        qs_ref, kvs_ref, *c_refs = rest[:rider.n_in]
        rest = rest[rider.n_in:]
        *qkv_refs, ma_ref, sgb_ref, vn_ref, os_ref, slab_ref = rest
        lin = pl.program_id(0) * pl.num_programs(1) + pl.program_id(1)
        _ride(rider, lin, qs_ref, kvs_ref, c_refs, os_ref, which=(0,))
    else:
        *qkv_refs, ma_ref, sgb_ref, vn_ref, slab_ref = rest
    qkv0_ref, qkv1_ref, qkv2_ref, *kvt_refs = qkv_refs
    t = x_ref.shape[1]
    xn = _rmsnorm(x_ref[0], g_ref[...]).astype(BF16)
    q = _project(xn, w_ref, 0, ATT_W) * (HEAD_DIM ** -0.5)
    kv = _project(xn, w_ref, ATT_W, 3 * ATT_W)
    for g, kvt_ref in enumerate(kvt_refs):
        tail_rows = slice(t - kvt_ref.shape[-1], t)
        kvt_ref[0, 0] = kv[tail_rows, g * ATT_G:(g + 1) * ATT_G].T
        kvt_ref[0, 1] = kv[tail_rows, ATT_W + g * ATT_G:ATT_W + (g + 1) * ATT_G].T
    per_slab = ATT_G // LANES
    for g, ((_, dil), out_ref) in enumerate(zip(DIL_GROUPS, (qkv0_ref, qkv1_ref, qkv2_ref))):
        secs = [q[:, g * ATT_G:(g + 1) * ATT_G], kv[:, g * ATT_G:(g + 1) * ATT_G],
                kv[:, ATT_W + g * ATT_G:ATT_W + (g + 1) * ATT_G]]
        if dil == 1:
            out_ref[0, 0] = jnp.concatenate(secs, axis=1).astype(BF16)
            continue
        for i, sec in enumerate(secs):
            for k in range(per_slab):
                slab_ref[i * per_slab + k] = sec[:, k * LANES:(k + 1) * LANES]
        for r in range(dil):
            pieces = [slab_ref[n, pl.ds(r, t // dil, stride=dil), :] for n in range(3 * per_slab)]
            out_ref[0, r] = jnp.concatenate(pieces, axis=1).astype(BF16)

    if rider is not None:
        _ride(rider, lin, qs_ref, kvs_ref, c_refs, os_ref, which=range(1, rider.per_step))
    u = _project(xn, w_ref, 3 * ATT_W, 3 * ATT_W + D_A)
    va = _project(xn, w_ref, 3 * ATT_W + D_A, 3 * ATT_W + 2 * D_A)
    vn = _layernorm(_gelu(va), lng_ref[...], lnb_ref[...])
    vn_ref[0] = vn[t - CHUNK:, :]
    vnb = vn.astype(BF16)
    row = lax.broadcasted_iota(jnp.int32, (CHUNK, CHUNK), 0)
    col = lax.broadcasted_iota(jnp.int32, (CHUNK, CHUNK), 1)
    w_m = [jnp.where(row >= col, wsp_ref[g], 0.0).astype(BF16) for g in range(N_GA)]
    rows = []
    for c in range(t // CHUNK):
        parts = [_dot(w_m[g], vnb[c * CHUNK:(c + 1) * CHUNK, g * GA_CH:(g + 1) * GA_CH]) + bsp_ref[g]
                 for g in range(N_GA)]
        rows.append(jnp.concatenate(parts, axis=1))
    mixed = jnp.concatenate(rows, axis=0)
    a = (_gelu(u) * mixed).astype(BF16)
    ma, sgb = _gates_and_branch_a(xn, w_ref, a, wpa_ref, d_model)
    ma_ref[0] = ma.astype(BF16)
    sgb_ref[0] = sgb.astype(BF16)


def _cast_chunk_rows(rows, steps):
    return next(c for c in range(BF16_SUBLANES, rows + 1, BF16_SUBLANES) if rows % c == 0 and rows // c <= steps)


def _mix_in(x, norm_mix, w_in, ln_g, ln_b, w_sp, b_sp, w_pa, *, tile, rider, sample):
    b, s, d = x.shape
    n_t = s // tile
    in_cols = w_in.shape[1]

    def tail_spec(win):
        w_blk = min(win, tile)
        first = n_t - win // w_blk
        return pl.BlockSpec((1, 2, ATT_G, w_blk), lambda i, j: (i, 0, 0, jnp.maximum(j - first, 0)))

    kern = functools.partial(_mix_in_kernel, d_model=d, rider=rider)
    in_specs = [
        pl.BlockSpec((1, tile, d), lambda i, j: (i, j, 0)),
        _const_spec((1, d)),
        _const_spec((d, in_cols)),
        _const_spec((1, D_A)),
        _const_spec((1, D_A)),
        _const_spec((N_GA, CHUNK, CHUNK)),
        _const_spec((N_GA, CHUNK, 1)),
        _const_spec((D_A, d)),
    ]
    out_specs = [
        pl.BlockSpec((1, dil, tile // dil, 3 * ATT_G), lambda i, j: (i, 0, j, 0)) for _, dil in DIL_GROUPS
    ] + [tail_spec(win) for win, _ in DIL_GROUPS] + [
        pl.BlockSpec((1, tile, d), lambda i, j: (i, j, 0)),
        pl.BlockSpec((1, tile, d), lambda i, j: (i, j, 0)),
        pl.BlockSpec((1, CHUNK, D_A), lambda i, j: (i, 0, 0)),
    ]
    out_shape = [
        jax.ShapeDtypeStruct((b, dil, s // dil, 3 * ATT_G), BF16) for _, dil in DIL_GROUPS
    ] + [jax.ShapeDtypeStruct((b, 2, ATT_G, win), F32) for win, _ in DIL_GROUPS] + [
        jax.ShapeDtypeStruct((b, s, d), BF16),
        jax.ShapeDtypeStruct((b, s, d), BF16),
        jax.ShapeDtypeStruct((b, CHUNK, D_A), F32),
    ]
    args = [x, norm_mix, w_in, ln_g, ln_b, w_sp, b_sp, w_pa]
    if rider is not None:
        r_in, r_out, r_shape, r_args = _rider_operands(rider, lambda i, j: i * n_t + j, *sample)
        in_specs, out_specs, out_shape, args = in_specs + r_in, out_specs + [r_out], out_shape + [r_shape], args + r_args
    return pl.pallas_call(
        kern,
        grid=(b, n_t),
        in_specs=in_specs,
        out_specs=out_specs,
        out_shape=out_shape,
        scratch_shapes=[pltpu.VMEM((3 * ATT_G // LANES, tile, LANES), F32)],
        compiler_params=_params("arbitrary", "arbitrary"),
        name="prompt_mix_in",
    )(*args)


def _band_attn_kernel(cur_ref, halo_ref, *rest, rider, n_cast):
    cast_in, rest = rest[:n_cast], rest[n_cast:]
    if rider is not None:
        qs_ref, kvs_ref, *c_refs = rest[:rider.n_in]
        rest = rest[rider.n_in:]
        o_ref, *cast_out, os_ref = rest
        lin = (pl.program_id(0) * pl.num_programs(1) + pl.program_id(1)) * pl.num_programs(2) + pl.program_id(2)
        _ride(rider, lin, qs_ref, kvs_ref, c_refs, os_ref)
    else:
        o_ref, *cast_out = rest
    for src, dst in zip(cast_in, cast_out):
        dst[...] = src[...].astype(BF16)
    rows = cur_ref.shape[2]
    first = pl.program_id(2) == 0
    cur = cur_ref[0, 0]
    halo = halo_ref[0, 0]
    q_all = cur[:, 0:ATT_G]
    k_all = jnp.concatenate([halo[:, ATT_G:2 * ATT_G], cur[:, ATT_G:2 * ATT_G]], axis=0)
    v_all = jnp.concatenate([halo[:, 2 * ATT_G:], cur[:, 2 * ATT_G:]], axis=0)
    lane_head = lax.broadcasted_iota(jnp.int32, (1, ATT_G), 1) // HEAD_DIM
    zero = jnp.zeros((), BF16)
    q_h = [jnp.where(lane_head == h, q_all, zero) for h in range(N_HG)]
    qi = lax.broadcasted_iota(jnp.int32, (N_HG * BAND, 2 * BAND), 0) % BAND
    ki = lax.broadcasted_iota(jnp.int32, (N_HG * BAND, 2 * BAND), 1)
    band = jnp.logical_and(ki >= qi, ki <= qi + BAND)
    band_first = jnp.logical_and(band, jnp.logical_or(ki >= BAND, jnp.logical_not(first)))
    n_qb = rows // BAND
    tiles = []
    for qb in range(n_qb):
        lhs = jnp.concatenate([q[qb * BAND:(qb + 1) * BAND] for q in q_h], axis=0)
        s = _dot_nt(lhs, k_all[qb * BAND:(qb + 2) * BAND])
        tiles.append(jnp.where(band_first if qb == 0 else band, s, -jnp.inf))
    s_all = jnp.concatenate(tiles, axis=0)
    m = jnp.max(s_all, axis=-1, keepdims=True)
    p = jnp.exp(s_all - m)
    l = jnp.sum(p, axis=-1, keepdims=True)
    p_n = (p * (1.0 / l)).astype(BF16)
    lse = m + jnp.log(l)
    for qb in range(n_qb):
        u0 = qb * N_HG * BAND
        pv = _dot(p_n[u0:u0 + N_HG * BAND], v_all[qb * BAND:(qb + 2) * BAND])
        o = pv[(N_HG - 1) * BAND:]
        lse_row = jnp.broadcast_to(lse[u0 + (N_HG - 1) * BAND:u0 + N_HG * BAND], (BAND, ATT_G))
        for h in range(N_HG - 2, -1, -1):
            o = jnp.where(lane_head == h, pv[h * BAND:(h + 1) * BAND], o)
            lse_row = jnp.where(lane_head == h, lse[u0 + h * BAND:u0 + (h + 1) * BAND], lse_row)
        o_ref[0, 0, qb * BAND:(qb + 1) * BAND, :] = jnp.concatenate([o, lse_row], axis=1)


def _band_attn(qkv_g, cast_ws, *, rows, rider, sample):
    b, d, m, w = qkv_g.shape
    per = rows // BAND
    n_j = m // rows
    lin = lambda i, r, j: (i * d + r) * n_j + j

    def cast_spec(wt):
        n_rows, cols = wt.shape
        c = _cast_chunk_rows(n_rows, b * d * n_j)
        return pl.BlockSpec((c, cols), lambda i, r, j: (jnp.minimum(lin(i, r, j), n_rows // c - 1), 0))

    in_specs = [
        pl.BlockSpec((1, 1, rows, w), lambda i, r, j: (i, r, j, 0)),
        pl.BlockSpec((1, 1, BAND, w), lambda i, r, j: (i, r, jnp.maximum(j * per - 1, 0), 0)),
    ] + [cast_spec(wt) for wt in cast_ws]
    out_specs = [pl.BlockSpec((1, 1, rows, 2 * ATT_G), lambda i, r, j: (i, r, j, 0))] + [cast_spec(wt) for wt in cast_ws]
    out_shape = [jax.ShapeDtypeStruct((b, d, m, 2 * ATT_G), F32)] + [jax.ShapeDtypeStruct(wt.shape, BF16) for wt in cast_ws]
    args = [qkv_g, qkv_g, *cast_ws]
    if rider is not None:
        r_in, r_out, r_shape, r_args = _rider_operands(rider, lin, *sample)
        in_specs, out_specs, out_shape, args = in_specs + r_in, out_specs + [r_out], out_shape + [r_shape], args + r_args
    return pl.pallas_call(
        functools.partial(_band_attn_kernel, rider=rider, n_cast=len(cast_ws)),
        grid=(b, d, n_j),
        in_specs=in_specs,
        out_specs=out_specs,
        out_shape=out_shape,
        compiler_params=_params("arbitrary", "arbitrary", "arbitrary"),
        name=f"prompt_band_attn_d{d}",
    )(*args)


def _merge_ffn_kernel(x_ref, ma_ref, sgb_ref, o0_ref, o1_ref, o2_ref, wpb_ref, wout_ref, nf_ref,
                      wup_ref, cw_ref, cb_ref, wdn_ref, nfin_ref, *rest, d_ff, rider):
    if rider is not None:
        qs_ref, kvs_ref, *c_refs = rest[:rider.n_in]
        y_ref, ct_ref, os_ref, carry_ref, hist_ref, tok_ref = rest[rider.n_in:]
    else:
        y_ref, ct_ref, carry_ref, hist_ref, tok_ref = rest
    t = x_ref.shape[1]

    @pl.when(pl.program_id(1) == 0)
    def _():
        carry_ref[...] = jnp.zeros_like(carry_ref)

    n_slab = 2 * ATT_G // LANES
    o_list = []
    for (_, dil), o_ref in zip(DIL_GROUPS, (o0_ref, o1_ref, o2_ref)):
        if dil == 1:
            o_list.append(o_ref[0, 0])
            continue
        for r in range(dil):
            for k in range(n_slab):
                tok_ref[k, pl.ds(r, t // dil, stride=dil), :] = o_ref[0, r, :, k * LANES:(k + 1) * LANES]
        o_list.append(jnp.concatenate([tok_ref[k] for k in range(n_slab)], axis=1))
    b_out = _combine_groups(o_list).astype(BF16)
    merged = ma_ref[0].astype(F32) + sgb_ref[0].astype(F32) * _dot(b_out, wpb_ref[...])
    h = x_ref[0] + _dot(merged.astype(BF16), wout_ref[...])
    hn = _rmsnorm(h, nf_ref[...]).astype(BF16)
    if rider is not None:
        _ride(rider, pl.program_id(0) * pl.num_programs(1) + pl.program_id(1), qs_ref, kvs_ref, c_refs, os_ref)

    def history(u, lo):
        prev1, prev2 = [], []
        for k in range(u.shape[1] // LANES):
            cols = slice(lo + k * LANES, lo + (k + 1) * LANES)
            hist_ref[k, pl.ds(0, 8, stride=2), :] = carry_ref[:, cols]
            hist_ref[k, pl.ds(16, t, stride=2), :] = u[:, k * LANES:(k + 1) * LANES]
            carry_ref[:, cols] = u[t - 8:, k * LANES:(k + 1) * LANES]
            prev1.append(hist_ref[k, pl.ds(14, t, stride=2), :])
            prev2.append(hist_ref[k, pl.ds(12, t, stride=2), :])
        return jnp.concatenate(prev1, axis=1), jnp.concatenate(prev2, axis=1)

    ffn = _conv_ffn(hn, history, wup_ref, cw_ref, cb_ref, wdn_ref, d_ff)
    ct_ref[0] = carry_ref[8 - (CONV_W - 1):, :]
    y_ref[0] = _rmsnorm(h + ffn, nfin_ref[...])


def _merge_ffn(x, ma, sgb, o_list, w_pb, w_out, norm_ffn, w_up, conv_w, conv_b, w_dn, norm_final, *, tile,
               rider, sample):
    b, s, d = x.shape
    d_ff = w_dn.shape[0]
    n_t = s // tile
    tok = lambda w: pl.BlockSpec((1, tile, w), lambda i, j: (i, j, 0))
    kern = functools.partial(_merge_ffn_kernel, d_ff=d_ff, rider=rider)
    in_specs = [tok(d), tok(d), tok(d)] + [
        pl.BlockSpec((1, dil, tile // dil, 2 * ATT_G), lambda i, j: (i, 0, j, 0)) for _, dil in DIL_GROUPS
    ] + [
        _const_spec((ATT_G, d)), _const_spec((d, d)), _const_spec((1, d)),
        _const_spec((d, 2 * d_ff)), _const_spec((CONV_W, 2 * d_ff)), _const_spec((1, 2 * d_ff)),
        _const_spec((d_ff, d)), _const_spec((1, d))]
    out_specs = [tok(d), pl.BlockSpec((1, CONV_W - 1, 2 * d_ff), lambda i, j: (i, 0, 0))]
    out_shape = [jax.ShapeDtypeStruct((b, s, d), F32), jax.ShapeDtypeStruct((b, CONV_W - 1, 2 * d_ff), F32)]
    args = [x, ma, sgb, *o_list, w_pb, w_out, norm_ffn, w_up, conv_w, conv_b, w_dn, norm_final]
    if rider is not None:
        r_in, r_out, r_shape, r_args = _rider_operands(rider, lambda i, j: i * n_t + j, *sample)
        in_specs, out_specs, out_shape, args = in_specs + r_in, out_specs + [r_out], out_shape + [r_shape], args + r_args
    return pl.pallas_call(
        kern,
        grid=(b, n_t),
        in_specs=in_specs,
        out_specs=out_specs,
        out_shape=out_shape,
        scratch_shapes=[pltpu.VMEM((8, 2 * d_ff), F32),
                        pltpu.VMEM((FF_BLK // LANES, 2 * (tile + 8), LANES), F32),
                        pltpu.VMEM((2 * ATT_G // LANES, tile, LANES), F32)],
        compiler_params=_params("arbitrary", "arbitrary"),
        name="prompt_merge_ffn",
    )(*args)


def _sample_in_kernel(x_ref, g_ref, w_ref, lng_ref, lnb_ref, w00_ref, b0_ref, wpa_ref,
                      q_ref, kv_ref, vn_ref, ma_ref, sgb_ref, wb_ref, wpab_ref, *, d_model):
    xn = _rmsnorm(x_ref[...], g_ref[...]).astype(BF16)
    q_ref[...] = _project(xn, w_ref, 0, ATT_W, wb_ref) * (HEAD_DIM ** -0.5)
    kv_ref[...] = _project(xn, w_ref, ATT_W, 3 * ATT_W, wb_ref)
    u = _project(xn, w_ref, 3 * ATT_W, 3 * ATT_W + D_A, wb_ref)
    va = _project(xn, w_ref, 3 * ATT_W + D_A, 3 * ATT_W + 2 * D_A, wb_ref)
    vn = _layernorm(_gelu(va), lng_ref[...], lnb_ref[...])
    vn_ref[...] = vn
    mixed = w00_ref[...] * vn + b0_ref[...]
    a = (_gelu(u) * mixed).astype(BF16)
    ma, sgb = _gates_and_branch_a(xn, w_ref, a, wpa_ref, d_model, wb_ref, wpab_ref)
    ma_ref[...] = ma
    sgb_ref[...] = sgb


def _sample_in(x, norm_mix, w_in, ln_g, ln_b, w00, b0, w_pa):
    n, d = x.shape
    in_cols = w_in.shape[1]
    kern = functools.partial(_sample_in_kernel, d_model=d)
    full = lambda *shape: pl.BlockSpec(shape, lambda: (0,) * len(shape))
    return pl.pallas_call(
        kern,
        in_specs=[full(n, d), full(1, d), full(d, in_cols), full(1, D_A), full(1, D_A),
                  full(1, D_A), full(1, D_A), full(D_A, d)],
        out_specs=[full(n, ATT_W), full(n, 2 * ATT_W), full(n, D_A), full(n, d), full(n, d),
                   full(d, in_cols), full(D_A, d)],
        out_shape=[jax.ShapeDtypeStruct((n, ATT_W), F32), jax.ShapeDtypeStruct((n, 2 * ATT_W), F32),
                   jax.ShapeDtypeStruct((n, D_A), F32), jax.ShapeDtypeStruct((n, d), F32),
                   jax.ShapeDtypeStruct((n, d), F32),
                   jax.ShapeDtypeStruct((d, in_cols), BF16), jax.ShapeDtypeStruct((D_A, d), BF16)],
        compiler_params=pltpu.CompilerParams(vmem_limit_bytes=VMEM_LIMIT_BYTES),
        name="sample_in",
    )(x, norm_mix, w_in, ln_g, ln_b, w00, b0, w_pa)


def _sample_attn_kernel(qs_ref, kvs_ref, *rest, rider):
    *c_refs, os_ref = rest
    _ride(rider, pl.program_id(0), qs_ref, kvs_ref, c_refs, os_ref)


def _sample_attn(rider, sample):
    r_in, r_out, r_shape, r_args = _rider_operands(rider, lambda i: i, *sample)
    return pl.pallas_call(
        functools.partial(_sample_attn_kernel, rider=rider),
        grid=(rider.steps,),
        in_specs=r_in,
        out_specs=r_out,
        out_shape=r_shape,
        compiler_params=_params("arbitrary"),
        name="sample_cache_attn",
    )(*r_args)


def _sample_ffn_kernel(x_ref, ma_ref, sgb_ref, b_ref, h0_ref, h1_ref, wpb_ref, wout_ref, nf_ref,
                       wup_ref, cw_ref, cb_ref, wdn_ref, nfin_ref, y_ref, up_ref, *, d_ff):
    merged = ma_ref[...] + sgb_ref[...] * _dot(b_ref[...].astype(BF16), wpb_ref[...])
    h = x_ref[...] + _dot(merged.astype(BF16), wout_ref[...])
    hn = _rmsnorm(h, nf_ref[...]).astype(BF16)

    def history(u, lo):
        up_ref[:, lo:lo + u.shape[1]] = u
        return h1_ref[:, lo:lo + u.shape[1]], h0_ref[:, lo:lo + u.shape[1]]

    ffn = _conv_ffn(hn, history, wup_ref, cw_ref, cb_ref, wdn_ref, d_ff)
    y_ref[...] = _rmsnorm(h + ffn, nfin_ref[...])


def _sample_ffn(x, ma, sgb, b_out, hist0, hist1, w_pb, w_out, norm_ffn, w_up, conv_w, conv_b, w_dn, norm_final):
    n, d = x.shape
    d_ff = w_dn.shape[0]
    kern = functools.partial(_sample_ffn_kernel, d_ff=d_ff)
    full = lambda *shape: pl.BlockSpec(shape, lambda: (0,) * len(shape))
    return pl.pallas_call(
        kern,
        in_specs=[full(n, d), full(n, d), full(n, d), full(n, ATT_G), full(n, 2 * d_ff), full(n, 2 * d_ff),
                  full(ATT_G, d), full(d, d), full(1, d), full(d, 2 * d_ff), full(CONV_W, 2 * d_ff),
                  full(1, 2 * d_ff), full(d_ff, d), full(1, d)],
        out_specs=[full(n, d), full(n, 2 * d_ff)],
        out_shape=[jax.ShapeDtypeStruct((n, d), F32), jax.ShapeDtypeStruct((n, 2 * d_ff), F32)],
        compiler_params=pltpu.CompilerParams(vmem_limit_bytes=VMEM_LIMIT_BYTES),
        name="sample_merge_ffn",
    )(x, ma, sgb, b_out, hist0, hist1, w_pb, w_out, norm_ffn, w_up, conv_w, conv_b, w_dn, norm_final)


def kernel(x_prompt, x_sample, cache_kv_w128, cache_kv_w512, cache_kv_w2048, state_conv_ffn, norm_mix, w_in, ln_v_gain, ln_v_bias, w_spatial, b_spatial, w_proj_a, w_proj_b, w_out, norm_ffn, w_up, conv_w, conv_b, w_down, norm_final):
    b, s, d = x_prompt.shape
    n_s, t_s, _ = x_sample.shape
    depth = w_in.shape[0]
    caches = (cache_kv_w128, cache_kv_w512, cache_kv_w2048)
    max_win = max(w for w, _ in DIL_GROUPS)
    assert depth == 1 and t_s == 1, "one layer, one new token per sample sequence"
    assert s % max_win == 0, "prompt length must cover whole strided query blocks of every dilation group"
    for (win, _), c in zip(DIL_GROUPS, caches):
        assert c.shape[2] == win, "cache must hold the whole window"

    ln_g = ln_v_gain[0][None]
    ln_b = ln_v_bias[0][None]
    conv_b2 = conv_b[0][None]
    nfin = norm_final[None]

    w00 = jnp.repeat(w_spatial[0][:, 0, 0], GA_CH)[None]
    b0 = jnp.repeat(b_spatial[0][:, 0], GA_CH)[None]
    q_s, kv_s, vn_s, ma_s, sgb_s, w_in_b, w_pa_b = _sample_in(
        x_sample[:, 0], norm_mix, w_in[0], ln_g, ln_b, w00, b0, w_proj_a[0])
    caches_t = [c[0].transpose(0, 2, 3, 4, 1).reshape(n_s, 2, ATT_G, c.shape[2]) for c in caches]
    sample = (q_s, kv_s, caches_t)

    mix_tile = ffn_tile = 512
    band_rows = [min(512, s // dil) for _, dil in DIL_GROUPS]
    hosts = [(b * (s // mix_tile), MIX_RIDERS_PER_STEP), (b * (s // ffn_tile), 1)] + [
        (b * (s // rows), 1) for rows in band_rows]
    riders, taken = [], 0
    for steps, per_step in hosts:
        count = min(steps * per_step, n_s - taken)
        riders.append(_Rider(taken, count, steps, per_step) if count > 0 else None)
        taken += count
    mix_rider, ffn_rider, *band_riders = riders

    mix_outs = _mix_in(
        x_prompt, norm_mix, w_in_b, ln_g, ln_b, w_spatial[0], b_spatial[0][:, :, None], w_pa_b,
        tile=mix_tile, rider=mix_rider, sample=sample)
    qkv_groups, kv_tails = mix_outs[:N_DIL], mix_outs[N_DIL:2 * N_DIL]
    ma, sgb, vn_last, *mix_part = mix_outs[2 * N_DIL:]
    ffn_ws = [w_proj_b[0], w_out[0], w_up[0], w_down[0]]
    o_list, band_parts = [], []
    for g, (qkv_g, rows, rider) in enumerate(zip(qkv_groups, band_rows, band_riders)):
        cast_ws = ffn_ws if g == N_DIL - 1 else []
        o_g, *extra = _band_attn(qkv_g, cast_ws, rows=rows, rider=rider, sample=sample)
        o_list.append(o_g)
        if cast_ws:
            (w_pb_b, w_out_b, w_up_b, w_dn_b), extra = extra[:len(cast_ws)], extra[len(cast_ws):]
        band_parts.extend(extra)
    y_prompt, conv_prompt, *ffn_part = _merge_ffn(x_prompt, ma, sgb, o_list, w_pb_b, w_out_b, norm_ffn, w_up_b,
                                                  conv_w[0], conv_b2, w_dn_b, nfin, tile=ffn_tile,
                                                  rider=ffn_rider, sample=sample)
    b_parts = mix_part + ffn_part + band_parts
    if taken < n_s:
        b_parts.append(_sample_attn(_Rider(taken, n_s - taken, n_s - taken, 1), sample))
    b_s = jnp.concatenate(b_parts, axis=0)

    kv_prompt = [t.reshape(b, 2, N_HG, HEAD_DIM, t.shape[-1]).transpose(0, 4, 1, 2, 3)[None] for t in kv_tails]

    hist = state_conv_ffn[0]
    y_s, up_s = _sample_ffn(x_sample[:, 0], ma_s, sgb_s, b_s, hist[:, 0], hist[:, 1],
                            w_pb_b, w_out_b, norm_ffn, w_up_b, conv_w[0], conv_b2, w_dn_b, nfin)

    kv_sample = [jnp.stack([kv_s[:, g * ATT_G:(g + 1) * ATT_G], kv_s[:, ATT_W + g * ATT_G: ATT_W + (g + 1) * ATT_G]],
                           axis=1).reshape(1, n_s, 1, 2, N_HG, HEAD_DIM) for g in range(N_DIL)]
    conv_sample = jnp.concatenate([hist[:, CONV_W - 2:], up_s[:, None]], axis=1)[None]

    return (y_prompt, y_s[:, None], kv_prompt[0], kv_sample[0], kv_prompt[1], kv_sample[1],
            kv_prompt[2], kv_sample[2], vn_last[None], vn_s[None, :, None], conv_prompt[None], conv_sample)
```

```python
import functools
import math
from typing import NamedTuple

import jax
import jax.numpy as jnp
from jax import lax
from jax.experimental import pallas as pl
from jax.experimental.pallas import tpu as pltpu

F32 = jnp.float32
BF16 = jnp.bfloat16

HEAD_DIM = 64
N_HG = 4
ATT_G = N_HG * HEAD_DIM
DIL_GROUPS = ((128, 1), (512, 4), (2048, 16))
N_DIL = len(DIL_GROUPS)
ATT_W = N_DIL * ATT_G
BAND = 128
CHUNK = 128
N_GA = 4
GA_CH = 128
D_A = N_GA * GA_CH
CONV_W = 3
EPS = 1e-6
GELU_C = math.sqrt(2.0 / math.pi)

LANES = 128
F32_SUBLANES = 8
BF16_SUBLANES = 16
VMEM_BYTES = 64 * 1024 * 1024
VMEM_LIMIT_BYTES = VMEM_BYTES - 2 * 1024 * 1024

MIX_TILE = 512
FFN_TILE = 512
BAND_ROWS = 1024
FF_BLK = 768
MIX_RIDERS_PER_STEP = 2


def _gelu(x):
    return x * (0.5 * (1.0 + jnp.tanh(GELU_C * (x + 0.044715 * (x * x * x)))))


def _rmsnorm(x, g):
    return x * lax.rsqrt(jnp.mean(x * x, axis=-1, keepdims=True) + EPS) * g


def _layernorm(x, g, b):
    mu = jnp.mean(x, axis=-1, keepdims=True)
    xc = x - mu
    var = jnp.mean(xc * xc, axis=-1, keepdims=True)
    return xc * lax.rsqrt(var + EPS) * g + b


def _dot(a, b):
    return jnp.dot(a, b, preferred_element_type=F32)


def _dot_nt(a, b):
    return lax.dot_general(a, b, (((1,), (1,)), ((), ())), preferred_element_type=F32)


def _const_spec(shape):
    nd = len(shape)
    return pl.BlockSpec(shape, lambda *_: (0,) * nd, pipeline_mode=pl.Buffered(1))


def _params(*sem):
    return pltpu.CompilerParams(dimension_semantics=sem, vmem_limit_bytes=VMEM_LIMIT_BYTES)


def _project(xn, w_ref, lo, hi, bf16_copy_ref=None):
    w = w_ref[:, lo:hi]
    if bf16_copy_ref is not None:
        w = w.astype(BF16)
        bf16_copy_ref[:, lo:hi] = w
    return _dot(xn, w)


def _gates_and_branch_a(xn, w_ref, a_bf16, wpa_ref, d_model, w_copy_ref=None, wpa_copy_ref=None):
    base = 3 * ATT_W + 2 * D_A
    pa = _project(a_bf16, wpa_ref, 0, d_model, wpa_copy_ref)
    ga = _project(xn, w_ref, base, base + d_model, w_copy_ref)
    gb = _project(xn, w_ref, base + d_model, base + 2 * d_model, w_copy_ref)
    return jax.nn.sigmoid(ga) * pa, jax.nn.sigmoid(gb)


def _combine_groups(o_list):
    lses = [o[:, ATT_G:] for o in o_list]
    m = functools.reduce(jnp.maximum, lses)
    ws = [jnp.exp(l - m) for l in lses]
    den = functools.reduce(lambda a, b: a + b, ws)
    num = functools.reduce(lambda a, b: a + b, [w * o[:, :ATT_G] for w, o in zip(ws, o_list)])
    return num / den


def _conv_ffn(hn, history_fn, wup_ref, cw_ref, cb_ref, wdn_ref, d_ff):
    acc = None
    for j0 in range(0, d_ff, FF_BLK):
        wid = min(FF_BLK, d_ff - j0)
        halves = []
        for base in (0, d_ff):
            lo = base + j0
            u = _dot(hn, wup_ref[:, lo:lo + wid])
            prev1, prev2 = history_fn(u, lo)
            c = (cb_ref[:, lo:lo + wid] + cw_ref[2:3, lo:lo + wid] * u
                 + cw_ref[0:1, lo:lo + wid] * prev2
                 + cw_ref[1:2, lo:lo + wid] * prev1)
            halves.append(c)
        act = (_gelu(halves[0]) * halves[1]).astype(BF16)
        part = _dot(act, wdn_ref[j0:j0 + wid, :])
        acc = part if acc is None else acc + part
    return acc


def _cache_attend(q_row, kv_row, c_refs):
    sub = lax.broadcasted_iota(jnp.int32, (8, ATT_G), 0)
    lane_head = lax.broadcasted_iota(jnp.int32, (8, ATT_G), 1) // HEAD_DIM
    diag = sub == lane_head
    s_g, s_new, v_new = [], [], []
    for g, (c_ref, (_, dil)) in enumerate(zip(c_refs, DIL_GROUPS)):
        q_bd = jnp.where(diag, q_row[:, g * ATT_G:(g + 1) * ATT_G], 0.0)
        s = _dot(q_bd.astype(BF16), c_ref[0, 0].astype(BF16))
        pos = lax.broadcasted_iota(jnp.int32, s.shape, 1)
        s_g.append(jnp.where(pos % dil == 0, s, -jnp.inf))
        s_new.append(jnp.sum(q_bd * kv_row[:, g * ATT_G:(g + 1) * ATT_G], axis=-1, keepdims=True))
        v_new.append(kv_row[:, ATT_W + g * ATT_G:ATT_W + (g + 1) * ATT_G])
    s_all = jnp.concatenate(s_g, axis=1)
    m = functools.reduce(jnp.maximum, s_new + [jnp.max(s_all, axis=-1, keepdims=True)])
    p = jnp.exp(s_all - m)
    p_new = [jnp.exp(sn - m) for sn in s_new]
    den = functools.reduce(lambda a, b: a + b, p_new + [jnp.sum(p, axis=-1, keepdims=True)])
    cols = []
    for h in range(N_HG):
        acc, lo = None, 0
        for c_ref in c_refs:
            for c in range(c_ref.shape[-1] // LANES):
                term = (c_ref[0, 1, h * HEAD_DIM:(h + 1) * HEAD_DIM, c * LANES:(c + 1) * LANES]
                        * p[h:h + 1, lo:lo + LANES])
                acc = term if acc is None else acc + term
                lo += LANES
        cols.append(jnp.sum(acc, axis=1, keepdims=True))
    cached = jnp.broadcast_to(jnp.concatenate(cols, axis=0), (ATT_G, LANES)).T[0:1, :]
    new = functools.reduce(lambda a, b: a + b, [pn * vn for pn, vn in zip(p_new, v_new)])
    new_row = jnp.sum(jnp.where(diag, new, 0.0), axis=0, keepdims=True)
    den_row = jnp.sum(jnp.where(diag, den, 0.0), axis=0, keepdims=True)
    return (cached + new_row) / den_row


class _Rider(NamedTuple):
    offset: int
    count: int
    steps: int
    per_step: int

    @property
    def n_in(self):
        return 2 + N_DIL * self.per_step


def _rider_operands(rider, lin_fn, q_s, kv_s, caches_t):
    n = q_s.shape[0]

    def cache_map(k, *ids):
        return (jnp.minimum(rider.offset + lin_fn(*ids) * rider.per_step + k, n - 1), 0, 0, 0)

    in_specs = [_const_spec(q_s.shape), _const_spec(kv_s.shape)]
    args = [q_s, kv_s]
    for k in range(rider.per_step):
        in_specs += [pl.BlockSpec((1,) + c.shape[1:], functools.partial(cache_map, k)) for c in caches_t]
        args += caches_t
    out_spec = pl.BlockSpec((rider.count, ATT_G), lambda *ids: (0, 0))
    out_shape = jax.ShapeDtypeStruct((rider.count, ATT_G), F32)
    return in_specs, out_spec, out_shape, args


def _ride(rider, lin, qs_ref, kvs_ref, c_refs, os_ref, which=None):
    for k in range(rider.per_step) if which is None else which:
        if k >= rider.per_step:
            continue
        local = lin * rider.per_step + k

        def work(local=local, k=k):
            seq = rider.offset + local
            os_ref[pl.ds(local, 1), :] = _cache_attend(qs_ref[pl.ds(seq, 1), :], kvs_ref[pl.ds(seq, 1), :],
                                                       c_refs[k * N_DIL:(k + 1) * N_DIL])

        if rider.count == rider.steps * rider.per_step:
            work()
        else:
            pl.when(local < rider.count)(work)


def _mix_in_kernel(x_ref, g_ref, w_ref, lng_ref, lnb_ref, wsp_ref, bsp_ref, wpa_ref, *rest, d_model, rider):
    if rider is not None:
        qs_ref, kvs_ref, *c_refs = rest[:rider.n_in]
        rest = rest[rider.n_in:]
        *qkv_refs, ma_ref, sgb_ref, vn_ref, os_ref, slab_ref = rest
        lin = pl.program_id(0) * pl.num_programs(1) + pl.program_id(1)
        _ride(rider, lin, qs_ref, kvs_ref, c_refs, os_ref, which=(0,))
    else:
        *qkv_refs, ma_ref, sgb_ref, vn_ref, slab_ref = rest
    qkv0_ref, qkv1_ref, qkv2_ref, *kvt_refs = qkv_refs
    t = x_ref.shape[1]
    xn = _rmsnorm(x_ref[0], g_ref[...]).astype(BF16)
    q = _project(xn, w_ref, 0, ATT_W) * (HEAD_DIM ** -0.5)
    kv = _project(xn, w_ref, ATT_W, 3 * ATT_W)
    for g, kvt_ref in enumerate(kvt_refs):
        tail_rows = slice(t - kvt_ref.shape[-1], t)
        kvt_ref[0, 0] = kv[tail_rows, g * ATT_G:(g + 1) * ATT_G].T
        kvt_ref[0, 1] = kv[tail_rows, ATT_W + g * ATT_G:ATT_W + (g + 1) * ATT_G].T
    per_slab = ATT_G // LANES
    for g, ((_, dil), out_ref) in enumerate(zip(DIL_GROUPS, (qkv0_ref, qkv1_ref, qkv2_ref))):
        secs = [q[:, g * ATT_G:(g + 1) * ATT_G], kv[:, g * ATT_G:(g + 1) * ATT_G],
                kv[:, ATT_W + g * ATT_G:ATT_W + (g + 1) * ATT_G]]
        if dil == 1:
            out_ref[0, 0] = jnp.concatenate(secs, axis=1).astype(BF16)
            continue
        for i, sec in enumerate(secs):
            for k in range(per_slab):
                slab_ref[i * per_slab + k] = sec[:, k * LANES:(k + 1) * LANES]
        for r in range(dil):
            pieces = [slab_ref[n, pl.ds(r, t // dil, stride=dil), :] for n in range(3 * per_slab)]
            out_ref[0, r] = jnp.concatenate(pieces, axis=1).astype(BF16)

    if rider is not None:
        _ride(rider, lin, qs_ref, kvs_ref, c_refs, os_ref, which=range(1, rider.per_step))
    u = _project(xn, w_ref, 3 * ATT_W, 3 * ATT_W + D_A)
    va = _project(xn, w_ref, 3 * ATT_W + D_A, 3 * ATT_W + 2 * D_A)
    vn = _layernorm(_gelu(va), lng_ref[...], lnb_ref[...])
    vn_ref[0] = vn[t - CHUNK:, :]
    vnb = vn.astype(BF16)
    row = lax.broadcasted_iota(jnp.int32, (CHUNK, CHUNK), 0)
    col = lax.broadcasted_iota(jnp.int32, (CHUNK, CHUNK), 1)
    w_m = [jnp.where(row >= col, wsp_ref[g], 0.0).astype(BF16) for g in range(N_GA)]
    rows = []
    for c in range(t // CHUNK):
        parts = [_dot(w_m[g], vnb[c * CHUNK:(c + 1) * CHUNK, g * GA_CH:(g + 1) * GA_CH]) + bsp_ref[g]
                 for g in range(N_GA)]
        rows.append(jnp.concatenate(parts, axis=1))
    mixed = jnp.concatenate(rows, axis=0)
    a = (_gelu(u) * mixed).astype(BF16)
    ma, sgb = _gates_and_branch_a(xn, w_ref, a, wpa_ref, d_model)
    ma_ref[0] = ma.astype(BF16)
    sgb_ref[0] = sgb.astype(BF16)


def _cast_chunk_rows(rows, steps):
    return next(c for c in range(BF16_SUBLANES, rows + 1, BF16_SUBLANES) if rows % c == 0 and rows // c <= steps)


def _mix_in(x, norm_mix, w_in, ln_g, ln_b, w_sp, b_sp, w_pa, *, tile, rider, sample):
    b, s, d = x.shape
    n_t = s // tile
    in_cols = w_in.shape[1]

    def tail_spec(win):
        w_blk = min(win, tile)
        first = n_t - win // w_blk
        return pl.BlockSpec((1, 2, ATT_G, w_blk), lambda i, j: (i, 0, 0, jnp.maximum(j - first, 0)))

    kern = functools.partial(_mix_in_kernel, d_model=d, rider=rider)
    in_specs = [
        pl.BlockSpec((1, tile, d), lambda i, j: (i, j, 0)),
        _const_spec((1, d)),
        _const_spec((d, in_cols)),
        _const_spec((1, D_A)),
        _const_spec((1, D_A)),
        _const_spec((N_GA, CHUNK, CHUNK)),
        _const_spec((N_GA, CHUNK, 1)),
        _const_spec((D_A, d)),
    ]
    out_specs = [
        pl.BlockSpec((1, dil, tile // dil, 3 * ATT_G), lambda i, j: (i, 0, j, 0)) for _, dil in DIL_GROUPS
    ] + [tail_spec(win) for win, _ in DIL_GROUPS] + [
        pl.BlockSpec((1, tile, d), lambda i, j: (i, j, 0)),
        pl.BlockSpec((1, tile, d), lambda i, j: (i, j, 0)),
        pl.BlockSpec((1, CHUNK, D_A), lambda i, j: (i, 0, 0)),
    ]
    out_shape = [
        jax.ShapeDtypeStruct((b, dil, s // dil, 3 * ATT_G), BF16) for _, dil in DIL_GROUPS
    ] + [jax.ShapeDtypeStruct((b, 2, ATT_G, win), F32) for win, _ in DIL_GROUPS] + [
        jax.ShapeDtypeStruct((b, s, d), BF16),
        jax.ShapeDtypeStruct((b, s, d), BF16),
        jax.ShapeDtypeStruct((b, CHUNK, D_A), F32),
    ]
    args = [x, norm_mix, w_in, ln_g, ln_b, w_sp, b_sp, w_pa]
    if rider is not None:
        r_in, r_out, r_shape, r_args = _rider_operands(rider, lambda i, j: i * n_t + j, *sample)
        in_specs, out_specs, out_shape, args = in_specs + r_in, out_specs + [r_out], out_shape + [r_shape], args + r_args
    return pl.pallas_call(
        kern,
        grid=(b, n_t),
        in_specs=in_specs,
        out_specs=out_specs,
        out_shape=out_shape,
        scratch_shapes=[pltpu.VMEM((3 * ATT_G // LANES, tile, LANES), F32)],
        compiler_params=_params("arbitrary", "arbitrary"),
        name="prompt_mix_in",
    )(*args)


def _band_attn_kernel(cur_ref, halo_ref, *rest, rider, n_cast):
    cast_in, rest = rest[:n_cast], rest[n_cast:]
    if rider is not None:
        qs_ref, kvs_ref, *c_refs = rest[:rider.n_in]
        rest = rest[rider.n_in:]
        o_ref, *cast_out, os_ref = rest
        lin = (pl.program_id(0) * pl.num_programs(1) + pl.program_id(1)) * pl.num_programs(2) + pl.program_id(2)
        _ride(rider, lin, qs_ref, kvs_ref, c_refs, os_ref)
    else:
        o_ref, *cast_out = rest
    for src, dst in zip(cast_in, cast_out):
        dst[...] = src[...].astype(BF16)
    rows = cur_ref.shape[2]
    first = pl.program_id(2) == 0
    cur = cur_ref[0, 0]
    halo = halo_ref[0, 0]
    q_all = cur[:, 0:ATT_G]
    k_all = jnp.concatenate([halo[:, ATT_G:2 * ATT_G], cur[:, ATT_G:2 * ATT_G]], axis=0)
    v_all = jnp.concatenate([halo[:, 2 * ATT_G:], cur[:, 2 * ATT_G:]], axis=0)
    lane_head = lax.broadcasted_iota(jnp.int32, (1, ATT_G), 1) // HEAD_DIM
    zero = jnp.zeros((), BF16)
    q_h = [jnp.where(lane_head == h, q_all, zero) for h in range(N_HG)]
    qi = lax.broadcasted_iota(jnp.int32, (N_HG * BAND, 2 * BAND), 0) % BAND
    ki = lax.broadcasted_iota(jnp.int32, (N_HG * BAND, 2 * BAND), 1)
    band = jnp.logical_and(ki >= qi, ki <= qi + BAND)
    band_first = jnp.logical_and(band, jnp.logical_or(ki >= BAND, jnp.logical_not(first)))
    n_qb = rows // BAND
    tiles = []
    for qb in range(n_qb):
        lhs = jnp.concatenate([q[qb * BAND:(qb + 1) * BAND] for q in q_h], axis=0)
        s = _dot_nt(lhs, k_all[qb * BAND:(qb + 2) * BAND])
        tiles.append(jnp.where(band_first if qb == 0 else band, s, -jnp.inf))
    s_all = jnp.concatenate(tiles, axis=0)
    m = jnp.max(s_all, axis=-1, keepdims=True)
    p = jnp.exp(s_all - m)
    l = jnp.sum(p, axis=-1, keepdims=True)
    p_n = (p * (1.0 / l)).astype(BF16)
    lse = m + jnp.log(l)
    for qb in range(n_qb):
        u0 = qb * N_HG * BAND
        pv = _dot(p_n[u0:u0 + N_HG * BAND], v_all[qb * BAND:(qb + 2) * BAND])
        o = pv[(N_HG - 1) * BAND:]
        lse_row = jnp.broadcast_to(lse[u0 + (N_HG - 1) * BAND:u0 + N_HG * BAND], (BAND, ATT_G))
        for h in range(N_HG - 2, -1, -1):
            o = jnp.where(lane_head == h, pv[h * BAND:(h + 1) * BAND], o)
            lse_row = jnp.where(lane_head == h, lse[u0 + h * BAND:u0 + (h + 1) * BAND], lse_row)
        o_ref[0, 0, qb * BAND:(qb + 1) * BAND, :] = jnp.concatenate([o, lse_row], axis=1)


def _band_attn(qkv_g, cast_ws, *, rows, rider, sample):
    b, d, m, w = qkv_g.shape
    per = rows // BAND
    n_j = m // rows
    lin = lambda i, r, j: (i * d + r) * n_j + j

    def cast_spec(wt):
        n_rows, cols = wt.shape
        c = _cast_chunk_rows(n_rows, b * d * n_j)
        return pl.BlockSpec((c, cols), lambda i, r, j: (jnp.minimum(lin(i, r, j), n_rows // c - 1), 0))

    in_specs = [
        pl.BlockSpec((1, 1, rows, w), lambda i, r, j: (i, r, j, 0)),
        pl.BlockSpec((1, 1, BAND, w), lambda i, r, j: (i, r, jnp.maximum(j * per - 1, 0), 0)),
    ] + [cast_spec(wt) for wt in cast_ws]
    out_specs = [pl.BlockSpec((1, 1, rows, 2 * ATT_G), lambda i, r, j: (i, r, j, 0))] + [cast_spec(wt) for wt in cast_ws]
    out_shape = [jax.ShapeDtypeStruct((b, d, m, 2 * ATT_G), F32)] + [jax.ShapeDtypeStruct(wt.shape, BF16) for wt in cast_ws]
    args = [qkv_g, qkv_g, *cast_ws]
    if rider is not None:
        r_in, r_out, r_shape, r_args = _rider_operands(rider, lin, *sample)
        in_specs, out_specs, out_shape, args = in_specs + r_in, out_specs + [r_out], out_shape + [r_shape], args + r_args
    return pl.pallas_call(
        functools.partial(_band_attn_kernel, rider=rider, n_cast=len(cast_ws)),
        grid=(b, d, n_j),
        in_specs=in_specs,
        out_specs=out_specs,
        out_shape=out_shape,
        compiler_params=_params("arbitrary", "arbitrary", "arbitrary"),
        name=f"prompt_band_attn_d{d}",
    )(*args)


def _merge_ffn_kernel(x_ref, ma_ref, sgb_ref, o0_ref, o1_ref, o2_ref, wpb_ref, wout_ref, nf_ref,
                      wup_ref, cw_ref, cb_ref, wdn_ref, nfin_ref, *rest, d_ff, rider):
    if rider is not None:
        qs_ref, kvs_ref, *c_refs = rest[:rider.n_in]
        y_ref, ct_ref, os_ref, carry_ref, hist_ref, tok_ref = rest[rider.n_in:]
    else:
        y_ref, ct_ref, carry_ref, hist_ref, tok_ref = rest
    t = x_ref.shape[1]

    @pl.when(pl.program_id(1) == 0)
    def _():
        carry_ref[...] = jnp.zeros_like(carry_ref)

    n_slab = 2 * ATT_G // LANES
    o_list = []
    for (_, dil), o_ref in zip(DIL_GROUPS, (o0_ref, o1_ref, o2_ref)):
        if dil == 1:
            o_list.append(o_ref[0, 0])
            continue
        for r in range(dil):
            for k in range(n_slab):
                tok_ref[k, pl.ds(r, t // dil, stride=dil), :] = o_ref[0, r, :, k * LANES:(k + 1) * LANES]
        o_list.append(jnp.concatenate([tok_ref[k] for k in range(n_slab)], axis=1))
    b_out = _combine_groups(o_list).astype(BF16)
    merged = ma_ref[0].astype(F32) + sgb_ref[0].astype(F32) * _dot(b_out, wpb_ref[...])
    h = x_ref[0] + _dot(merged.astype(BF16), wout_ref[...])
    hn = _rmsnorm(h, nf_ref[...]).astype(BF16)
    if rider is not None:
        _ride(rider, pl.program_id(0) * pl.num_programs(1) + pl.program_id(1), qs_ref, kvs_ref, c_refs, os_ref)

    def history(u, lo):
        prev1, prev2 = [], []
        for k in range(u.shape[1] // LANES):
            cols = slice(lo + k * LANES, lo + (k + 1) * LANES)
            hist_ref[k, pl.ds(0, 8, stride=2), :] = carry_ref[:, cols]
            hist_ref[k, pl.ds(16, t, stride=2), :] = u[:, k * LANES:(k + 1) * LANES]
            carry_ref[:, cols] = u[t - 8:, k * LANES:(k + 1) * LANES]
            prev1.append(hist_ref[k, pl.ds(14, t, stride=2), :])
            prev2.append(hist_ref[k, pl.ds(12, t, stride=2), :])
        return jnp.concatenate(prev1, axis=1), jnp.concatenate(prev2, axis=1)

    ffn = _conv_ffn(hn, history, wup_ref, cw_ref, cb_ref, wdn_ref, d_ff)
    ct_ref[0] = carry_ref[8 - (CONV_W - 1):, :]
    y_ref[0] = _rmsnorm(h + ffn, nfin_ref[...])


def _merge_ffn(x, ma, sgb, o_list, w_pb, w_out, norm_ffn, w_up, conv_w, conv_b, w_dn, norm_final, *, tile,
               rider, sample):
    b, s, d = x.shape
    d_ff = w_dn.shape[0]
    n_t = s // tile
    tok = lambda w: pl.BlockSpec((1, tile, w), lambda i, j: (i, j, 0))
    kern = functools.partial(_merge_ffn_kernel, d_ff=d_ff, rider=rider)
    in_specs = [tok(d), tok(d), tok(d)] + [
        pl.BlockSpec((1, dil, tile // dil, 2 * ATT_G), lambda i, j: (i, 0, j, 0)) for _, dil in DIL_GROUPS
    ] + [
        _const_spec((ATT_G, d)), _const_spec((d, d)), _const_spec((1, d)),
        _const_spec((d, 2 * d_ff)), _const_spec((CONV_W, 2 * d_ff)), _const_spec((1, 2 * d_ff)),
        _const_spec((d_ff, d)), _const_spec((1, d))]
    out_specs = [tok(d), pl.BlockSpec((1, CONV_W - 1, 2 * d_ff), lambda i, j: (i, 0, 0))]
    out_shape = [jax.ShapeDtypeStruct((b, s, d), F32), jax.ShapeDtypeStruct((b, CONV_W - 1, 2 * d_ff), F32)]
    args = [x, ma, sgb, *o_list, w_pb, w_out, norm_ffn, w_up, conv_w, conv_b, w_dn, norm_final]
    if rider is not None:
        r_in, r_out, r_shape, r_args = _rider_operands(rider, lambda i, j: i * n_t + j, *sample)
        in_specs, out_specs, out_shape, args = in_specs + r_in, out_specs + [r_out], out_shape + [r_shape], args + r_args
    return pl.pallas_call(
        kern,
        grid=(b, n_t),
        in_specs=in_specs,
        out_specs=out_specs,
        out_shape=out_shape,
        scratch_shapes=[pltpu.VMEM((8, 2 * d_ff), F32),
                        pltpu.VMEM((FF_BLK // LANES, 2 * (tile + 8), LANES), F32),
                        pltpu.VMEM((2 * ATT_G // LANES, tile, LANES), F32)],
        compiler_params=_params("arbitrary", "arbitrary"),
        name="prompt_merge_ffn",
    )(*args)


def _sample_in_kernel(x_ref, g_ref, w_ref, lng_ref, lnb_ref, w00_ref, b0_ref, wpa_ref,
                      q_ref, kv_ref, vn_ref, ma_ref, sgb_ref, wb_ref, wpab_ref, *, d_model):
    xn = _rmsnorm(x_ref[...], g_ref[...]).astype(BF16)
    q_ref[...] = _project(xn, w_ref, 0, ATT_W, wb_ref) * (HEAD_DIM ** -0.5)
    kv_ref[...] = _project(xn, w_ref, ATT_W, 3 * ATT_W, wb_ref)
    u = _project(xn, w_ref, 3 * ATT_W, 3 * ATT_W + D_A, wb_ref)
    va = _project(xn, w_ref, 3 * ATT_W + D_A, 3 * ATT_W + 2 * D_A, wb_ref)
    vn = _layernorm(_gelu(va), lng_ref[...], lnb_ref[...])
    vn_ref[...] = vn
    mixed = w00_ref[...] * vn + b0_ref[...]
    a = (_gelu(u) * mixed).astype(BF16)
    ma, sgb = _gates_and_branch_a(xn, w_ref, a, wpa_ref, d_model, wb_ref, wpab_ref)
    ma_ref[...] = ma
    sgb_ref[...] = sgb


def _sample_in(x, norm_mix, w_in, ln_g, ln_b, w00, b0, w_pa):
    n, d = x.shape
    in_cols = w_in.shape[1]
    kern = functools.partial(_sample_in_kernel, d_model=d)
    full = lambda *shape: pl.BlockSpec(shape, lambda: (0,) * len(shape))
    return pl.pallas_call(
        kern,
        in_specs=[full(n, d), full(1, d), full(d, in_cols), full(1, D_A), full(1, D_A),
                  full(1, D_A), full(1, D_A), full(D_A, d)],
        out_specs=[full(n, ATT_W), full(n, 2 * ATT_W), full(n, D_A), full(n, d), full(n, d),
                   full(d, in_cols), full(D_A, d)],
        out_shape=[jax.ShapeDtypeStruct((n, ATT_W), F32), jax.ShapeDtypeStruct((n, 2 * ATT_W), F32),
                   jax.ShapeDtypeStruct((n, D_A), F32), jax.ShapeDtypeStruct((n, d), F32),
                   jax.ShapeDtypeStruct((n, d), F32),
                   jax.ShapeDtypeStruct((d, in_cols), BF16), jax.ShapeDtypeStruct((D_A, d), BF16)],
        compiler_params=pltpu.CompilerParams(vmem_limit_bytes=VMEM_LIMIT_BYTES),
        name="sample_in",
    )(x, norm_mix, w_in, ln_g, ln_b, w00, b0, w_pa)


def _sample_attn_kernel(qs_ref, kvs_ref, *rest, rider):
    *c_refs, os_ref = rest
    _ride(rider, pl.program_id(0), qs_ref, kvs_ref, c_refs, os_ref)


def _sample_attn(rider, sample):
    r_in, r_out, r_shape, r_args = _rider_operands(rider, lambda i: i, *sample)
    return pl.pallas_call(
        functools.partial(_sample_attn_kernel, rider=rider),
        grid=(rider.steps,),
        in_specs=r_in,
        out_specs=r_out,
        out_shape=r_shape,
        compiler_params=_params("arbitrary"),
        name="sample_cache_attn",
    )(*r_args)


def _sample_ffn_kernel(x_ref, ma_ref, sgb_ref, b_ref, h0_ref, h1_ref, wpb_ref, wout_ref, nf_ref,
                       wup_ref, cw_ref, cb_ref, wdn_ref, nfin_ref, y_ref, up_ref, *, d_ff):
    merged = ma_ref[...] + sgb_ref[...] * _dot(b_ref[...].astype(BF16), wpb_ref[...])
    h = x_ref[...] + _dot(merged.astype(BF16), wout_ref[...])
    hn = _rmsnorm(h, nf_ref[...]).astype(BF16)

    def history(u, lo):
        up_ref[:, lo:lo + u.shape[1]] = u
        return h1_ref[:, lo:lo + u.shape[1]], h0_ref[:, lo:lo + u.shape[1]]

    ffn = _conv_ffn(hn, history, wup_ref, cw_ref, cb_ref, wdn_ref, d_ff)
    y_ref[...] = _rmsnorm(h + ffn, nfin_ref[...])


def _sample_ffn(x, ma, sgb, b_out, hist0, hist1, w_pb, w_out, norm_ffn, w_up, conv_w, conv_b, w_dn, norm_final):
    n, d = x.shape
    d_ff = w_dn.shape[0]
    kern = functools.partial(_sample_ffn_kernel, d_ff=d_ff)
    full = lambda *shape: pl.BlockSpec(shape, lambda: (0,) * len(shape))
    return pl.pallas_call(
        kern,
        in_specs=[full(n, d), full(n, d), full(n, d), full(n, ATT_G), full(n, 2 * d_ff), full(n, 2 * d_ff),
                  full(ATT_G, d), full(d, d), full(1, d), full(d, 2 * d_ff), full(CONV_W, 2 * d_ff),
                  full(1, 2 * d_ff), full(d_ff, d), full(1, d)],
        out_specs=[full(n, d), full(n, 2 * d_ff)],
        out_shape=[jax.ShapeDtypeStruct((n, d), F32), jax.ShapeDtypeStruct((n, 2 * d_ff), F32)],
        compiler_params=pltpu.CompilerParams(vmem_limit_bytes=VMEM_LIMIT_BYTES),
        name="sample_merge_ffn",
    )(x, ma, sgb, b_out, hist0, hist1, w_pb, w_out, norm_ffn, w_up, conv_w, conv_b, w_dn, norm_final)


def kernel(x_prompt, x_sample, cache_kv_w128, cache_kv_w512, cache_kv_w2048, state_conv_ffn, norm_mix, w_in, ln_v_gain, ln_v_bias, w_spatial, b_spatial, w_proj_a, w_proj_b, w_out, norm_ffn, w_up, conv_w, conv_b, w_down, norm_final):
    b, s, d = x_prompt.shape
    n_s, t_s, _ = x_sample.shape
    depth = w_in.shape[0]
    caches = (cache_kv_w128, cache_kv_w512, cache_kv_w2048)
    max_win = max(w for w, _ in DIL_GROUPS)
    assert depth == 1 and t_s == 1, "one layer, one new token per sample sequence"
    assert s % max_win == 0, "prompt length must cover whole strided query blocks of every dilation group"
    for (win, _), c in zip(DIL_GROUPS, caches):
        assert c.shape[2] == win, "cache must hold the whole window"

    ln_g = ln_v_gain[0][None]
    ln_b = ln_v_bias[0][None]
    conv_b2 = conv_b[0][None]
    nfin = norm_final[None]

    w00 = jnp.repeat(w_spatial[0][:, 0, 0], GA_CH)[None]
    b0 = jnp.repeat(b_spatial[0][:, 0], GA_CH)[None]
    q_s, kv_s, vn_s, ma_s, sgb_s, w_in_b, w_pa_b = _sample_in(
        x_sample[:, 0], norm_mix, w_in[0], ln_g, ln_b, w00, b0, w_proj_a[0])
    caches_t = [c[0].transpose(0, 2, 3, 4, 1).reshape(n_s, 2, ATT_G, c.shape[2]) for c in caches]
    sample = (q_s, kv_s, caches_t)

    mix_tile, ffn_tile = MIX_TILE, FFN_TILE
    band_rows = [min(BAND_ROWS, s // dil) for _, dil in DIL_GROUPS]
    hosts = [(b * (s // mix_tile), MIX_RIDERS_PER_STEP), (b * (s // ffn_tile), 1)] + [
        (b * (s // rows), 1) for rows in band_rows]
    riders, taken = [], 0
    for steps, per_step in hosts:
        count = min(steps * per_step, n_s - taken)
        riders.append(_Rider(taken, count, steps, per_step) if count > 0 else None)
        taken += count
    mix_rider, ffn_rider, *band_riders = riders

    mix_outs = _mix_in(
        x_prompt, norm_mix, w_in_b, ln_g, ln_b, w_spatial[0], b_spatial[0][:, :, None], w_pa_b,
        tile=mix_tile, rider=mix_rider, sample=sample)
    qkv_groups, kv_tails = mix_outs[:N_DIL], mix_outs[N_DIL:2 * N_DIL]
    ma, sgb, vn_last, *mix_part = mix_outs[2 * N_DIL:]
    ffn_ws = [w_proj_b[0], w_out[0], w_up[0], w_down[0]]
    o_list, band_parts = [], []
    for g, (qkv_g, rows, rider) in enumerate(zip(qkv_groups, band_rows, band_riders)):
        cast_ws = ffn_ws if g == N_DIL - 1 else []
        o_g, *extra = _band_attn(qkv_g, cast_ws, rows=rows, rider=rider, sample=sample)
        o_list.append(o_g)
        if cast_ws:
            (w_pb_b, w_out_b, w_up_b, w_dn_b), extra = extra[:len(cast_ws)], extra[len(cast_ws):]
        band_parts.extend(extra)
    y_prompt, conv_prompt, *ffn_part = _merge_ffn(x_prompt, ma, sgb, o_list, w_pb_b, w_out_b, norm_ffn, w_up_b,
                                                  conv_w[0], conv_b2, w_dn_b, nfin, tile=ffn_tile,
                                                  rider=ffn_rider, sample=sample)
    b_parts = mix_part + ffn_part + band_parts
    if taken < n_s:
        b_parts.append(_sample_attn(_Rider(taken, n_s - taken, n_s - taken, 1), sample))
    b_s = jnp.concatenate(b_parts, axis=0)

    kv_prompt = [t.reshape(b, 2, N_HG, HEAD_DIM, t.shape[-1]).transpose(0, 4, 1, 2, 3)[None] for t in kv_tails]

    hist = state_conv_ffn[0]
    y_s, up_s = _sample_ffn(x_sample[:, 0], ma_s, sgb_s, b_s, hist[:, 0], hist[:, 1],
                            w_pb_b, w_out_b, norm_ffn, w_up_b, conv_w[0], conv_b2, w_dn_b, nfin)

    kv_sample = [jnp.stack([kv_s[:, g * ATT_G:(g + 1) * ATT_G], kv_s[:, ATT_W + g * ATT_G: ATT_W + (g + 1) * ATT_G]],
                           axis=1).reshape(1, n_s, 1, 2, N_HG, HEAD_DIM) for g in range(N_DIL)]
    conv_sample = jnp.concatenate([hist[:, CONV_W - 2:], up_s[:, None]], axis=1)[None]

    return (y_prompt, y_s[:, None], kv_prompt[0], kv_sample[0], kv_prompt[1], kv_sample[1],
            kv_prompt[2], kv_sample[2], vn_last[None], vn_s[None, :, None], conv_prompt[None], conv_sample)
```

```python
import functools
import math
from typing import NamedTuple

import jax
import jax.numpy as jnp
from jax import lax
from jax.experimental import pallas as pl
from jax.experimental.pallas import tpu as pltpu

F32 = jnp.float32
BF16 = jnp.bfloat16

HEAD_DIM = 64
N_HG = 4
ATT_G = N_HG * HEAD_DIM
DIL_GROUPS = ((128, 1), (512, 4), (2048, 16))
N_DIL = len(DIL_GROUPS)
ATT_W = N_DIL * ATT_G
BAND = 128
CHUNK = 128
N_GA = 4
GA_CH = 128
D_A = N_GA * GA_CH
CONV_W = 3
EPS = 1e-6
GELU_C = math.sqrt(2.0 / math.pi)

LANES = 128
F32_SUBLANES = 8
BF16_SUBLANES = 16
VMEM_BYTES = 64 * 1024 * 1024
VMEM_LIMIT_BYTES = VMEM_BYTES - 2 * 1024 * 1024

MIX_TILE = 512
FFN_TILE = 512
BAND_ROWS = 1024
FF_BLK = 768
MIX_RIDERS_PER_STEP = 2


def _gelu(x):
    return x * (0.5 * (1.0 + jnp.tanh(GELU_C * (x + 0.044715 * (x * x * x)))))


def _rmsnorm(x, g):
    return x * lax.rsqrt(jnp.mean(x * x, axis=-1, keepdims=True) + EPS) * g


def _layernorm(x, g, b):
    mu = jnp.mean(x, axis=-1, keepdims=True)
    xc = x - mu
    var = jnp.mean(xc * xc, axis=-1, keepdims=True)
    return xc * lax.rsqrt(var + EPS) * g + b


def _dot(a, b):
    return jnp.dot(a, b, preferred_element_type=F32)


def _dot_nt(a, b):
    return lax.dot_general(a, b, (((1,), (1,)), ((), ())), preferred_element_type=F32)


def _const_spec(shape):
    nd = len(shape)
    return pl.BlockSpec(shape, lambda *_: (0,) * nd, pipeline_mode=pl.Buffered(1))


def _params(*sem):
    return pltpu.CompilerParams(dimension_semantics=sem, vmem_limit_bytes=VMEM_LIMIT_BYTES)


def _project(xn, w_ref, lo, hi, bf16_copy_ref=None):
    w = w_ref[:, lo:hi]
    if bf16_copy_ref is not None:
        w = w.astype(BF16)
        bf16_copy_ref[:, lo:hi] = w
    return _dot(xn, w)


def _gates_and_branch_a(xn, w_ref, a_bf16, wpa_ref, d_model, w_copy_ref=None, wpa_copy_ref=None):
    base = 3 * ATT_W + 2 * D_A
    pa = _project(a_bf16, wpa_ref, 0, d_model, wpa_copy_ref)
    ga = _project(xn, w_ref, base, base + d_model, w_copy_ref)
    gb = _project(xn, w_ref, base + d_model, base + 2 * d_model, w_copy_ref)
    return jax.nn.sigmoid(ga) * pa, jax.nn.sigmoid(gb)


def _combine_groups(o_list):
    lses = [o[:, ATT_G:] for o in o_list]
    m = functools.reduce(jnp.maximum, lses)
    ws = [jnp.exp(l - m) for l in lses]
    den = functools.reduce(lambda a, b: a + b, ws)
    num = functools.reduce(lambda a, b: a + b, [w * o[:, :ATT_G] for w, o in zip(ws, o_list)])
    return num / den


def _conv_ffn(hn, history_fn, wup_ref, cw_ref, cb_ref, wdn_ref, d_ff):
    acc = None
    for j0 in range(0, d_ff, FF_BLK):
        wid = min(FF_BLK, d_ff - j0)
        halves = []
        for base in (0, d_ff):
            lo = base + j0
            u = _dot(hn, wup_ref[:, lo:lo + wid])
            prev1, prev2 = history_fn(u, lo)
            c = (cb_ref[:, lo:lo + wid] + cw_ref[2:3, lo:lo + wid] * u
                 + cw_ref[0:1, lo:lo + wid] * prev2
                 + cw_ref[1:2, lo:lo + wid] * prev1)
            halves.append(c)
        act = (_gelu(halves[0]) * halves[1]).astype(BF16)
        part = _dot(act, wdn_ref[j0:j0 + wid, :])
        acc = part if acc is None else acc + part
    return acc


def _cache_attend(q_row, kv_row, c_refs):
    sub = lax.broadcasted_iota(jnp.int32, (F32_SUBLANES, ATT_G), 0)
    lane_head = lax.broadcasted_iota(jnp.int32, (F32_SUBLANES, ATT_G), 1) // HEAD_DIM
    diag = sub == lane_head
    s_g, s_new, v_new = [], [], []
    for g, (c_ref, (_, dil)) in enumerate(zip(c_refs, DIL_GROUPS)):
        q_bd = jnp.where(diag, q_row[:, g * ATT_G:(g + 1) * ATT_G], 0.0)
        s = _dot(q_bd.astype(BF16), c_ref[0, 0].astype(BF16))
        pos = lax.broadcasted_iota(jnp.int32, s.shape, 1)
        s_g.append(jnp.where(pos % dil == 0, s, -jnp.inf))
        s_new.append(jnp.sum(q_bd * kv_row[:, g * ATT_G:(g + 1) * ATT_G], axis=-1, keepdims=True))
        v_new.append(kv_row[:, ATT_W + g * ATT_G:ATT_W + (g + 1) * ATT_G])
    s_all = jnp.concatenate(s_g, axis=1)
    m = functools.reduce(jnp.maximum, s_new + [jnp.max(s_all, axis=-1, keepdims=True)])
    p = jnp.exp(s_all - m)
    p_new = [jnp.exp(sn - m) for sn in s_new]
    den = functools.reduce(lambda a, b: a + b, p_new + [jnp.sum(p, axis=-1, keepdims=True)])
    cols = []
    for h in range(N_HG):
        acc, lo = None, 0
        for c_ref in c_refs:
            for c in range(c_ref.shape[-1] // LANES):
                term = (c_ref[0, 1, h * HEAD_DIM:(h + 1) * HEAD_DIM, c * LANES:(c + 1) * LANES]
                        * p[h:h + 1, lo:lo + LANES])
                acc = term if acc is None else acc + term
                lo += LANES
        cols.append(jnp.sum(acc, axis=1, keepdims=True))
    cached = jnp.broadcast_to(jnp.concatenate(cols, axis=0), (ATT_G, LANES)).T[0:1, :]
    new = functools.reduce(lambda a, b: a + b, [pn * vn for pn, vn in zip(p_new, v_new)])
    new_row = jnp.sum(jnp.where(diag, new, 0.0), axis=0, keepdims=True)
    den_row = jnp.sum(jnp.where(diag, den, 0.0), axis=0, keepdims=True)
    return (cached + new_row) / den_row


class _Rider(NamedTuple):
    offset: int
    count: int
    steps: int
    per_step: int

    @property
    def n_in(self):
        return 2 + N_DIL * self.per_step


def _rider_operands(rider, lin_fn, q_s, kv_s, caches_t):
    n = q_s.shape[0]

    def cache_map(k, *ids):
        return (jnp.minimum(rider.offset + lin_fn(*ids) * rider.per_step + k, n - 1), 0, 0, 0)

    in_specs = [_const_spec(q_s.shape), _const_spec(kv_s.shape)]
    args = [q_s, kv_s]
    for k in range(rider.per_step):
        in_specs += [pl.BlockSpec((1,) + c.shape[1:], functools.partial(cache_map, k)) for c in caches_t]
        args += caches_t
    out_spec = pl.BlockSpec((rider.count, ATT_G), lambda *ids: (0, 0))
    out_shape = jax.ShapeDtypeStruct((rider.count, ATT_G), F32)
    return in_specs, out_spec, out_shape, args


def _ride(rider, lin, qs_ref, kvs_ref, c_refs, os_ref, which=None):
    for k in range(rider.per_step) if which is None else which:
        if k >= rider.per_step:
            continue
        local = lin * rider.per_step + k

        def work(local=local, k=k):
            seq = rider.offset + local
            os_ref[pl.ds(local, 1), :] = _cache_attend(qs_ref[pl.ds(seq, 1), :], kvs_ref[pl.ds(seq, 1), :],
                                                       c_refs[k * N_DIL:(k + 1) * N_DIL])

        if rider.count == rider.steps * rider.per_step:
            work()
        else:
            pl.when(local < rider.count)(work)


def _mix_in_kernel(x_ref, g_ref, w_ref, lng_ref, lnb_ref, wsp_ref, bsp_ref, wpa_ref, *rest, d_model, rider):
    if rider is not None:
        qs_ref, kvs_ref, *c_refs = rest[:rider.n_in]
        rest = rest[rider.n_in:]
        *qkv_refs, ma_ref, sgb_ref, vn_ref, os_ref, slab_ref = rest
        lin = pl.program_id(0) * pl.num_programs(1) + pl.program_id(1)
        _ride(rider, lin, qs_ref, kvs_ref, c_refs, os_ref, which=(0,))
    else:
        *qkv_refs, ma_ref, sgb_ref, vn_ref, slab_ref = rest
    qkv0_ref, qkv1_ref, qkv2_ref, *kvt_refs = qkv_refs
    t = x_ref.shape[1]
    xn = _rmsnorm(x_ref[0], g_ref[...]).astype(BF16)
    q = _project(xn, w_ref, 0, ATT_W) * (HEAD_DIM ** -0.5)
    kv = _project(xn, w_ref, ATT_W, 3 * ATT_W)
    for g, kvt_ref in enumerate(kvt_refs):
        tail_rows = slice(t - kvt_ref.shape[-1], t)
        kvt_ref[0, 0] = kv[tail_rows, g * ATT_G:(g + 1) * ATT_G].T
        kvt_ref[0, 1] = kv[tail_rows, ATT_W + g * ATT_G:ATT_W + (g + 1) * ATT_G].T
    per_slab = ATT_G // LANES
    for g, ((_, dil), out_ref) in enumerate(zip(DIL_GROUPS, (qkv0_ref, qkv1_ref, qkv2_ref))):
        secs = [q[:, g * ATT_G:(g + 1) * ATT_G], kv[:, g * ATT_G:(g + 1) * ATT_G],
                kv[:, ATT_W + g * ATT_G:ATT_W + (g + 1) * ATT_G]]
        if dil == 1:
            out_ref[0, 0] = jnp.concatenate(secs, axis=1).astype(BF16)
            continue
        for i, sec in enumerate(secs):
            for k in range(per_slab):
                slab_ref[i * per_slab + k] = sec[:, k * LANES:(k + 1) * LANES]
        for r in range(dil):
            pieces = [slab_ref[n, pl.ds(r, t // dil, stride=dil), :] for n in range(3 * per_slab)]
            out_ref[0, r] = jnp.concatenate(pieces, axis=1).astype(BF16)

    if rider is not None:
        _ride(rider, lin, qs_ref, kvs_ref, c_refs, os_ref, which=range(1, rider.per_step))
    u = _project(xn, w_ref, 3 * ATT_W, 3 * ATT_W + D_A)
    va = _project(xn, w_ref, 3 * ATT_W + D_A, 3 * ATT_W + 2 * D_A)
    vn = _layernorm(_gelu(va), lng_ref[...], lnb_ref[...])
    vn_ref[0] = vn[t - CHUNK:, :]
    vnb = vn.astype(BF16)
    row = lax.broadcasted_iota(jnp.int32, (CHUNK, CHUNK), 0)
    col = lax.broadcasted_iota(jnp.int32, (CHUNK, CHUNK), 1)
    w_m = [jnp.where(row >= col, wsp_ref[g], 0.0).astype(BF16) for g in range(N_GA)]
    rows = []
    for c in range(t // CHUNK):
        parts = [_dot(w_m[g], vnb[c * CHUNK:(c + 1) * CHUNK, g * GA_CH:(g + 1) * GA_CH]) + bsp_ref[g]
                 for g in range(N_GA)]
        rows.append(jnp.concatenate(parts, axis=1))
    mixed = jnp.concatenate(rows, axis=0)
    a = (_gelu(u) * mixed).astype(BF16)
    ma, sgb = _gates_and_branch_a(xn, w_ref, a, wpa_ref, d_model)
    ma_ref[0] = ma.astype(BF16)
    sgb_ref[0] = sgb.astype(BF16)


def _cast_chunk_rows(rows, steps):
    return next(c for c in range(BF16_SUBLANES, rows + 1, BF16_SUBLANES) if rows % c == 0 and rows // c <= steps)


def _mix_in(x, norm_mix, w_in, ln_g, ln_b, w_sp, b_sp, w_pa, *, tile, rider, sample):
    b, s, d = x.shape
    n_t = s // tile
    in_cols = w_in.shape[1]

    def tail_spec(win):
        w_blk = min(win, tile)
        first = n_t - win // w_blk
        return pl.BlockSpec((1, 2, ATT_G, w_blk), lambda i, j: (i, 0, 0, jnp.maximum(j - first, 0)))

    kern = functools.partial(_mix_in_kernel, d_model=d, rider=rider)
    in_specs = [
        pl.BlockSpec((1, tile, d), lambda i, j: (i, j, 0)),
        _const_spec((1, d)),
        _const_spec((d, in_cols)),
        _const_spec((1, D_A)),
        _const_spec((1, D_A)),
        _const_spec((N_GA, CHUNK, CHUNK)),
        _const_spec((N_GA, CHUNK, 1)),
        _const_spec((D_A, d)),
    ]
    out_specs = [
        pl.BlockSpec((1, dil, tile // dil, 3 * ATT_G), lambda i, j: (i, 0, j, 0)) for _, dil in DIL_GROUPS
    ] + [tail_spec(win) for win, _ in DIL_GROUPS] + [
        pl.BlockSpec((1, tile, d), lambda i, j: (i, j, 0)),
        pl.BlockSpec((1, tile, d), lambda i, j: (i, j, 0)),
        pl.BlockSpec((1, CHUNK, D_A), lambda i, j: (i, 0, 0)),
    ]
    out_shape = [
        jax.ShapeDtypeStruct((b, dil, s // dil, 3 * ATT_G), BF16) for _, dil in DIL_GROUPS
    ] + [jax.ShapeDtypeStruct((b, 2, ATT_G, win), F32) for win, _ in DIL_GROUPS] + [
        jax.ShapeDtypeStruct((b, s, d), BF16),
        jax.ShapeDtypeStruct((b, s, d), BF16),
        jax.ShapeDtypeStruct((b, CHUNK, D_A), F32),
    ]
    args = [x, norm_mix, w_in, ln_g, ln_b, w_sp, b_sp, w_pa]
    if rider is not None:
        r_in, r_out, r_shape, r_args = _rider_operands(rider, lambda i, j: i * n_t + j, *sample)
        in_specs, out_specs, out_shape, args = in_specs + r_in, out_specs + [r_out], out_shape + [r_shape], args + r_args
    return pl.pallas_call(
        kern,
        grid=(b, n_t),
        in_specs=in_specs,
        out_specs=out_specs,
        out_shape=out_shape,
        scratch_shapes=[pltpu.VMEM((3 * ATT_G // LANES, tile, LANES), F32)],
        compiler_params=_params("arbitrary", "arbitrary"),
        name="prompt_mix_in",
    )(*args)


def _band_attn_kernel(cur_ref, halo_ref, *rest, rider, n_cast):
    cast_in, rest = rest[:n_cast], rest[n_cast:]
    if rider is not None:
        qs_ref, kvs_ref, *c_refs = rest[:rider.n_in]
        rest = rest[rider.n_in:]
        o_ref, *cast_out, os_ref = rest
        lin = (pl.program_id(0) * pl.num_programs(1) + pl.program_id(1)) * pl.num_programs(2) + pl.program_id(2)
        _ride(rider, lin, qs_ref, kvs_ref, c_refs, os_ref)
    else:
        o_ref, *cast_out = rest
    for src, dst in zip(cast_in, cast_out):
        dst[...] = src[...].astype(BF16)
    n_res, rows = cur_ref.shape[1], cur_ref.shape[2]
    first = pl.program_id(2) == 0
    lane_head = lax.broadcasted_iota(jnp.int32, (1, ATT_G), 1) // HEAD_DIM
    zero = jnp.zeros((), BF16)
    qi = lax.broadcasted_iota(jnp.int32, (N_HG * BAND, 2 * BAND), 0) % BAND
    ki = lax.broadcasted_iota(jnp.int32, (N_HG * BAND, 2 * BAND), 1)
    band = jnp.logical_and(ki >= qi, ki <= qi + BAND)
    band_first = jnp.logical_and(band, jnp.logical_or(ki >= BAND, jnp.logical_not(first)))
    n_qb = rows // BAND
    tiles, v_alls = [], []
    for rr in range(n_res):
        cur = cur_ref[0, rr]
        halo = halo_ref[0, rr]
        q_all = cur[:, 0:ATT_G]
        k_all = jnp.concatenate([halo[:, ATT_G:2 * ATT_G], cur[:, ATT_G:2 * ATT_G]], axis=0)
        v_alls.append(jnp.concatenate([halo[:, 2 * ATT_G:], cur[:, 2 * ATT_G:]], axis=0))
        q_h = [jnp.where(lane_head == h, q_all, zero) for h in range(N_HG)]
        for qb in range(n_qb):
            lhs = jnp.concatenate([q[qb * BAND:(qb + 1) * BAND] for q in q_h], axis=0)
            s = _dot_nt(lhs, k_all[qb * BAND:(qb + 2) * BAND])
            tiles.append(jnp.where(band_first if qb == 0 else band, s, -jnp.inf))
    s_all = jnp.concatenate(tiles, axis=0)
    m = jnp.max(s_all, axis=-1, keepdims=True)
    p = jnp.exp(s_all - m)
    l = jnp.sum(p, axis=-1, keepdims=True)
    p_n = (p * (1.0 / l)).astype(BF16)
    lse = m + jnp.log(l)
    for rr in range(n_res):
        for qb in range(n_qb):
            u0 = (rr * n_qb + qb) * N_HG * BAND
            pv = _dot(p_n[u0:u0 + N_HG * BAND], v_alls[rr][qb * BAND:(qb + 2) * BAND])
            o = pv[(N_HG - 1) * BAND:]
            lse_row = jnp.broadcast_to(lse[u0 + (N_HG - 1) * BAND:u0 + N_HG * BAND], (BAND, ATT_G))
            for h in range(N_HG - 2, -1, -1):
                o = jnp.where(lane_head == h, pv[h * BAND:(h + 1) * BAND], o)
                lse_row = jnp.where(lane_head == h, lse[u0 + h * BAND:u0 + (h + 1) * BAND], lse_row)
            o_ref[0, rr, qb * BAND:(qb + 1) * BAND, :] = jnp.concatenate([o, lse_row], axis=1)


def _band_attn(qkv_g, cast_ws, *, rows, n_res, rider, sample):
    b, d, m, w = qkv_g.shape
    per = rows // BAND
    n_j = m // rows
    n_r = d // n_res
    lin = lambda i, r, j: (i * n_r + r) * n_j + j

    def cast_spec(wt):
        n_rows, cols = wt.shape
        c = _cast_chunk_rows(n_rows, b * n_r * n_j)
        return pl.BlockSpec((c, cols), lambda i, r, j: (jnp.minimum(lin(i, r, j), n_rows // c - 1), 0))

    in_specs = [
        pl.BlockSpec((1, n_res, rows, w), lambda i, r, j: (i, r, j, 0)),
        pl.BlockSpec((1, n_res, BAND, w), lambda i, r, j: (i, r, jnp.maximum(j * per - 1, 0), 0)),
    ] + [cast_spec(wt) for wt in cast_ws]
    out_specs = [pl.BlockSpec((1, n_res, rows, 2 * ATT_G), lambda i, r, j: (i, r, j, 0))] + [cast_spec(wt) for wt in cast_ws]
    out_shape = [jax.ShapeDtypeStruct((b, d, m, 2 * ATT_G), F32)] + [jax.ShapeDtypeStruct(wt.shape, BF16) for wt in cast_ws]
    args = [qkv_g, qkv_g, *cast_ws]
    if rider is not None:
        r_in, r_out, r_shape, r_args = _rider_operands(rider, lin, *sample)
        in_specs, out_specs, out_shape, args = in_specs + r_in, out_specs + [r_out], out_shape + [r_shape], args + r_args
    return pl.pallas_call(
        functools.partial(_band_attn_kernel, rider=rider, n_cast=len(cast_ws)),
        grid=(b, n_r, n_j),
        in_specs=in_specs,
        out_specs=out_specs,
        out_shape=out_shape,
        compiler_params=_params("arbitrary", "arbitrary", "arbitrary"),
        name=f"prompt_band_attn_d{d}",
    )(*args)


def _merge_ffn_kernel(x_ref, ma_ref, sgb_ref, o0_ref, o1_ref, o2_ref, wpb_ref, wout_ref, nf_ref,
                      wup_ref, cw_ref, cb_ref, wdn_ref, nfin_ref, *rest, d_ff, rider):
    if rider is not None:
        qs_ref, kvs_ref, *c_refs = rest[:rider.n_in]
        y_ref, ct_ref, os_ref, carry_ref, hist_ref, tok_ref = rest[rider.n_in:]
    else:
        y_ref, ct_ref, carry_ref, hist_ref, tok_ref = rest
    t = x_ref.shape[1]

    @pl.when(pl.program_id(1) == 0)
    def _():
        carry_ref[...] = jnp.zeros_like(carry_ref)

    n_slab = 2 * ATT_G // LANES
    o_list = []
    for (_, dil), o_ref in zip(DIL_GROUPS, (o0_ref, o1_ref, o2_ref)):
        if dil == 1:
            o_list.append(o_ref[0, 0])
            continue
        for r in range(dil):
            for k in range(n_slab):
                tok_ref[k, pl.ds(r, t // dil, stride=dil), :] = o_ref[0, r, :, k * LANES:(k + 1) * LANES]
        o_list.append(jnp.concatenate([tok_ref[k] for k in range(n_slab)], axis=1))
    b_out = _combine_groups(o_list).astype(BF16)
    merged = ma_ref[0].astype(F32) + sgb_ref[0].astype(F32) * _dot(b_out, wpb_ref[...])
    h = x_ref[0] + _dot(merged.astype(BF16), wout_ref[...])
    hn = _rmsnorm(h, nf_ref[...]).astype(BF16)
    if rider is not None:
        _ride(rider, pl.program_id(0) * pl.num_programs(1) + pl.program_id(1), qs_ref, kvs_ref, c_refs, os_ref)

    def history(u, lo):
        c = F32_SUBLANES
        prev1, prev2 = [], []
        for k in range(u.shape[1] // LANES):
            cols = slice(lo + k * LANES, lo + (k + 1) * LANES)
            hist_ref[k, pl.ds(0, c, stride=2), :] = carry_ref[:, cols]
            hist_ref[k, pl.ds(2 * c, t, stride=2), :] = u[:, k * LANES:(k + 1) * LANES]
            carry_ref[:, cols] = u[t - c:, k * LANES:(k + 1) * LANES]
            prev1.append(hist_ref[k, pl.ds(2 * (c - 1), t, stride=2), :])
            prev2.append(hist_ref[k, pl.ds(2 * (c - 2), t, stride=2), :])
        return jnp.concatenate(prev1, axis=1), jnp.concatenate(prev2, axis=1)

    ffn = _conv_ffn(hn, history, wup_ref, cw_ref, cb_ref, wdn_ref, d_ff)
    ct_ref[0] = carry_ref[F32_SUBLANES - (CONV_W - 1):, :]
    y_ref[0] = _rmsnorm(h + ffn, nfin_ref[...])


def _merge_ffn(x, ma, sgb, o_list, w_pb, w_out, norm_ffn, w_up, conv_w, conv_b, w_dn, norm_final, *, tile,
               rider, sample):
    b, s, d = x.shape
    d_ff = w_dn.shape[0]
    n_t = s // tile
    tok = lambda w: pl.BlockSpec((1, tile, w), lambda i, j: (i, j, 0))
    kern = functools.partial(_merge_ffn_kernel, d_ff=d_ff, rider=rider)
    in_specs = [tok(d), tok(d), tok(d)] + [
        pl.BlockSpec((1, dil, tile // dil, 2 * ATT_G), lambda i, j: (i, 0, j, 0)) for _, dil in DIL_GROUPS
    ] + [
        _const_spec((ATT_G, d)), _const_spec((d, d)), _const_spec((1, d)),
        _const_spec((d, 2 * d_ff)), _const_spec((CONV_W, 2 * d_ff)), _const_spec((1, 2 * d_ff)),
        _const_spec((d_ff, d)), _const_spec((1, d))]
    out_specs = [tok(d), pl.BlockSpec((1, CONV_W - 1, 2 * d_ff), lambda i, j: (i, 0, 0))]
    out_shape = [jax.ShapeDtypeStruct((b, s, d), F32), jax.ShapeDtypeStruct((b, CONV_W - 1, 2 * d_ff), F32)]
    args = [x, ma, sgb, *o_list, w_pb, w_out, norm_ffn, w_up, conv_w, conv_b, w_dn, norm_final]
    if rider is not None:
        r_in, r_out, r_shape, r_args = _rider_operands(rider, lambda i, j: i * n_t + j, *sample)
        in_specs, out_specs, out_shape, args = in_specs + r_in, out_specs + [r_out], out_shape + [r_shape], args + r_args
    return pl.pallas_call(
        kern,
        grid=(b, n_t),
        in_specs=in_specs,
        out_specs=out_specs,
        out_shape=out_shape,
        scratch_shapes=[pltpu.VMEM((F32_SUBLANES, 2 * d_ff), F32),
                        pltpu.VMEM((FF_BLK // LANES, 2 * (tile + F32_SUBLANES), LANES), F32),
                        pltpu.VMEM((2 * ATT_G // LANES, tile, LANES), F32)],
        compiler_params=_params("arbitrary", "arbitrary"),
        name="prompt_merge_ffn",
    )(*args)


def _sample_in_kernel(x_ref, g_ref, w_ref, lng_ref, lnb_ref, w00_ref, b0_ref, wpa_ref,
                      q_ref, kv_ref, vn_ref, ma_ref, sgb_ref, wb_ref, wpab_ref, *, d_model):
    xn = _rmsnorm(x_ref[...], g_ref[...]).astype(BF16)
    q_ref[...] = _project(xn, w_ref, 0, ATT_W, wb_ref) * (HEAD_DIM ** -0.5)
    kv_ref[...] = _project(xn, w_ref, ATT_W, 3 * ATT_W, wb_ref)
    u = _project(xn, w_ref, 3 * ATT_W, 3 * ATT_W + D_A, wb_ref)
    va = _project(xn, w_ref, 3 * ATT_W + D_A, 3 * ATT_W + 2 * D_A, wb_ref)
    vn = _layernorm(_gelu(va), lng_ref[...], lnb_ref[...])
    vn_ref[...] = vn
    mixed = w00_ref[...] * vn + b0_ref[...]
    a = (_gelu(u) * mixed).astype(BF16)
    ma, sgb = _gates_and_branch_a(xn, w_ref, a, wpa_ref, d_model, wb_ref, wpab_ref)
    ma_ref[...] = ma
    sgb_ref[...] = sgb


def _sample_in(x, norm_mix, w_in, ln_g, ln_b, w00, b0, w_pa):
    n, d = x.shape
    in_cols = w_in.shape[1]
    kern = functools.partial(_sample_in_kernel, d_model=d)
    full = lambda *shape: pl.BlockSpec(shape, lambda: (0,) * len(shape))
    return pl.pallas_call(
        kern,
        in_specs=[full(n, d), full(1, d), full(d, in_cols), full(1, D_A), full(1, D_A),
                  full(1, D_A), full(1, D_A), full(D_A, d)],
        out_specs=[full(n, ATT_W), full(n, 2 * ATT_W), full(n, D_A), full(n, d), full(n, d),
                   full(d, in_cols), full(D_A, d)],
        out_shape=[jax.ShapeDtypeStruct((n, ATT_W), F32), jax.ShapeDtypeStruct((n, 2 * ATT_W), F32),
                   jax.ShapeDtypeStruct((n, D_A), F32), jax.ShapeDtypeStruct((n, d), F32),
                   jax.ShapeDtypeStruct((n, d), F32),
                   jax.ShapeDtypeStruct((d, in_cols), BF16), jax.ShapeDtypeStruct((D_A, d), BF16)],
        compiler_params=pltpu.CompilerParams(vmem_limit_bytes=VMEM_LIMIT_BYTES),
        name="sample_in",
    )(x, norm_mix, w_in, ln_g, ln_b, w00, b0, w_pa)


def _sample_attn_kernel(qs_ref, kvs_ref, *rest, rider):
    *c_refs, os_ref = rest
    _ride(rider, pl.program_id(0), qs_ref, kvs_ref, c_refs, os_ref)


def _sample_attn(rider, sample):
    r_in, r_out, r_shape, r_args = _rider_operands(rider, lambda i: i, *sample)
    return pl.pallas_call(
        functools.partial(_sample_attn_kernel, rider=rider),
        grid=(rider.steps,),
        in_specs=r_in,
        out_specs=r_out,
        out_shape=r_shape,
        compiler_params=_params("arbitrary"),
        name="sample_cache_attn",
    )(*r_args)


def _sample_ffn_kernel(x_ref, ma_ref, sgb_ref, b_ref, h0_ref, h1_ref, wpb_ref, wout_ref, nf_ref,
                       wup_ref, cw_ref, cb_ref, wdn_ref, nfin_ref, y_ref, up_ref, *, d_ff):
    merged = ma_ref[...] + sgb_ref[...] * _dot(b_ref[...].astype(BF16), wpb_ref[...])
    h = x_ref[...] + _dot(merged.astype(BF16), wout_ref[...])
    hn = _rmsnorm(h, nf_ref[...]).astype(BF16)

    def history(u, lo):
        up_ref[:, lo:lo + u.shape[1]] = u
        return h1_ref[:, lo:lo + u.shape[1]], h0_ref[:, lo:lo + u.shape[1]]

    ffn = _conv_ffn(hn, history, wup_ref, cw_ref, cb_ref, wdn_ref, d_ff)
    y_ref[...] = _rmsnorm(h + ffn, nfin_ref[...])


def _sample_ffn(x, ma, sgb, b_out, hist0, hist1, w_pb, w_out, norm_ffn, w_up, conv_w, conv_b, w_dn, norm_final):
    n, d = x.shape
    d_ff = w_dn.shape[0]
    kern = functools.partial(_sample_ffn_kernel, d_ff=d_ff)
    full = lambda *shape: pl.BlockSpec(shape, lambda: (0,) * len(shape))
    return pl.pallas_call(
        kern,
        in_specs=[full(n, d), full(n, d), full(n, d), full(n, ATT_G), full(n, 2 * d_ff), full(n, 2 * d_ff),
                  full(ATT_G, d), full(d, d), full(1, d), full(d, 2 * d_ff), full(CONV_W, 2 * d_ff),
                  full(1, 2 * d_ff), full(d_ff, d), full(1, d)],
        out_specs=[full(n, d), full(n, 2 * d_ff)],
        out_shape=[jax.ShapeDtypeStruct((n, d), F32), jax.ShapeDtypeStruct((n, 2 * d_ff), F32)],
        compiler_params=pltpu.CompilerParams(vmem_limit_bytes=VMEM_LIMIT_BYTES),
        name="sample_merge_ffn",
    )(x, ma, sgb, b_out, hist0, hist1, w_pb, w_out, norm_ffn, w_up, conv_w, conv_b, w_dn, norm_final)


def kernel(x_prompt, x_sample, cache_kv_w128, cache_kv_w512, cache_kv_w2048, state_conv_ffn, norm_mix, w_in, ln_v_gain, ln_v_bias, w_spatial, b_spatial, w_proj_a, w_proj_b, w_out, norm_ffn, w_up, conv_w, conv_b, w_down, norm_final):
    b, s, d = x_prompt.shape
    n_s, t_s, _ = x_sample.shape
    depth = w_in.shape[0]
    caches = (cache_kv_w128, cache_kv_w512, cache_kv_w2048)
    max_win = max(w for w, _ in DIL_GROUPS)
    assert depth == 1 and t_s == 1, "one layer, one new token per sample sequence"
    assert s % max_win == 0, "prompt length must cover whole strided query blocks of every dilation group"
    for (win, _), c in zip(DIL_GROUPS, caches):
        assert c.shape[2] == win, "cache must hold the whole window"

    ln_g = ln_v_gain[0][None]
    ln_b = ln_v_bias[0][None]
    conv_b2 = conv_b[0][None]
    nfin = norm_final[None]

    w00 = jnp.repeat(w_spatial[0][:, 0, 0], GA_CH)[None]
    b0 = jnp.repeat(b_spatial[0][:, 0], GA_CH)[None]
    q_s, kv_s, vn_s, ma_s, sgb_s, w_in_b, w_pa_b = _sample_in(
        x_sample[:, 0], norm_mix, w_in[0], ln_g, ln_b, w00, b0, w_proj_a[0])
    caches_t = [c[0].transpose(0, 2, 3, 4, 1).reshape(n_s, 2, ATT_G, c.shape[2]) for c in caches]
    sample = (q_s, kv_s, caches_t)

    mix_tile, ffn_tile = MIX_TILE, FFN_TILE
    band_rows = [min(BAND_ROWS, s // dil) for _, dil in DIL_GROUPS]
    band_res = [math.gcd(dil, BAND_ROWS // rows) for (_, dil), rows in zip(DIL_GROUPS, band_rows)]
    hosts = [(b * (s // mix_tile), MIX_RIDERS_PER_STEP), (b * (s // ffn_tile), 1)] + [
        (b * (s // (rows * n_res)), 1) for rows, n_res in zip(band_rows, band_res)]
    riders, taken = [], 0
    for steps, per_step in hosts:
        count = min(steps * per_step, n_s - taken)
        riders.append(_Rider(taken, count, steps, per_step) if count > 0 else None)
        taken += count
    mix_rider, ffn_rider, *band_riders = riders

    mix_outs = _mix_in(
        x_prompt, norm_mix, w_in_b, ln_g, ln_b, w_spatial[0], b_spatial[0][:, :, None], w_pa_b,
        tile=mix_tile, rider=mix_rider, sample=sample)
    qkv_groups, kv_tails = mix_outs[:N_DIL], mix_outs[N_DIL:2 * N_DIL]
    ma, sgb, vn_last, *mix_part = mix_outs[2 * N_DIL:]
    ffn_ws = [w_proj_b[0], w_out[0], w_up[0], w_down[0]]
    o_list, band_parts = [], []
    for g, (qkv_g, rows, n_res, rider) in enumerate(zip(qkv_groups, band_rows, band_res, band_riders)):
        cast_ws = ffn_ws if g == N_DIL - 1 else []
        o_g, *extra = _band_attn(qkv_g, cast_ws, rows=rows, n_res=n_res, rider=rider, sample=sample)
        o_list.append(o_g)
        if cast_ws:
            (w_pb_b, w_out_b, w_up_b, w_dn_b), extra = extra[:len(cast_ws)], extra[len(cast_ws):]
        band_parts.extend(extra)
    y_prompt, conv_prompt, *ffn_part = _merge_ffn(x_prompt, ma, sgb, o_list, w_pb_b, w_out_b, norm_ffn, w_up_b,
                                                  conv_w[0], conv_b2, w_dn_b, nfin, tile=ffn_tile,
                                                  rider=ffn_rider, sample=sample)
    b_parts = mix_part + ffn_part + band_parts
    if taken < n_s:
        b_parts.append(_sample_attn(_Rider(taken, n_s - taken, n_s - taken, 1), sample))
    b_s = jnp.concatenate(b_parts, axis=0)

    kv_prompt = [t.reshape(b, 2, N_HG, HEAD_DIM, t.shape[-1]).transpose(0, 4, 1, 2, 3)[None] for t in kv_tails]

    hist = state_conv_ffn[0]
    y_s, up_s = _sample_ffn(x_sample[:, 0], ma_s, sgb_s, b_s, hist[:, 0], hist[:, 1],
                            w_pb_b, w_out_b, norm_ffn, w_up_b, conv_w[0], conv_b2, w_dn_b, nfin)

    kv_sample = [jnp.stack([kv_s[:, g * ATT_G:(g + 1) * ATT_G], kv_s[:, ATT_W + g * ATT_G: ATT_W + (g + 1) * ATT_G]],
                           axis=1).reshape(1, n_s, 1, 2, N_HG, HEAD_DIM) for g in range(N_DIL)]
    conv_sample = jnp.concatenate([hist[:, CONV_W - 2:], up_s[:, None]], axis=1)[None]

    return (y_prompt, y_s[:, None], kv_prompt[0], kv_sample[0], kv_prompt[1], kv_sample[1],
            kv_prompt[2], kv_sample[2], vn_last[None], vn_s[None, :, None], conv_prompt[None], conv_sample)
```
